```python
import jax, jax.numpy as jnp
from jax import lax
import numpy as np

D_MODEL = 4096
BATCH = 2
SEQ = 4096
DEPTH = 2

GRID_W = 64
N_MIXERS = 2
EPS = 1e-6
NEG = -1e30

ATT_HEADS = 32
ATT_KV_HEADS = 8
ATT_HEAD_DIM = D_MODEL // ATT_HEADS
ATT_GROUP = ATT_HEADS // ATT_KV_HEADS
ATT_Q_COLS = ATT_HEADS * ATT_HEAD_DIM
ATT_KV_COLS = ATT_KV_HEADS * ATT_HEAD_DIM
ATT_IN_COLS = ATT_Q_COLS + 2 * ATT_KV_COLS
Q_BLOCK = 128
ROPE_THETA = 10000.0

ML_HEADS = 8
ML_QK_DIM = D_MODEL // (2 * ML_HEADS)
ML_V_DIM = D_MODEL // ML_HEADS
ML_CHUNK = 64
GATE_CAP = 15.0
ML_QK_COLS = ML_HEADS * ML_QK_DIM
ML_V_COLS = ML_HEADS * ML_V_DIM
ML_GATE_COLS = 4 * ML_HEADS
ML_IN_COLS = 2 * ML_QK_COLS + 2 * ML_V_COLS + ML_GATE_COLS

D_FF_DENSE = (7 * D_MODEL) // 2
N_EXPERTS = 8
TOP_K = 2
D_FF_EXPERT = (11 * D_MODEL) // 8

kernel_name = "hybrid_axial_gqa_bi_mlstm_moe_encoder"


def rms_norm(x, g):
    xf = x.astype(jnp.float32)
    y = xf * lax.rsqrt(jnp.mean(xf * xf, axis=-1, keepdims=True) + EPS)
    return (y * g.astype(jnp.float32)).astype(x.dtype)


def axial_rope_tables(seq_len):
    rows = seq_len // GRID_W
    row_ids = jnp.repeat(jnp.arange(rows, dtype=jnp.float32), GRID_W)
    col_ids = jnp.tile(jnp.arange(GRID_W, dtype=jnp.float32), rows)
    n_freq = ATT_HEAD_DIM // 4
    inv_freq = ROPE_THETA ** (-jnp.arange(n_freq, dtype=jnp.float32) / n_freq)
    ang = jnp.concatenate([row_ids[:, None] * inv_freq, col_ids[:, None] * inv_freq], axis=-1)
    return jnp.cos(ang), jnp.sin(ang)


def apply_rope(x, cos, sin):
    B, S, H, Dh = x.shape
    xr = x.astype(jnp.float32).reshape(B, S, H, Dh // 2, 2)
    x0, x1 = xr[..., 0], xr[..., 1]
    c = cos[None, :, None, :]
    s = sin[None, :, None, :]
    out = jnp.stack([x0 * c - x1 * s, x0 * s + x1 * c], axis=-1).reshape(B, S, H, Dh)
    return out.astype(x.dtype)


def attention_mixer(xn, w_in, q_gain, k_gain, w_out):
    B, S, _ = xn.shape
    proj = xn @ w_in
    q, k, v = jnp.split(proj, [ATT_Q_COLS, ATT_Q_COLS + ATT_KV_COLS], axis=-1)
    q = rms_norm(q.reshape(B, S, ATT_HEADS, ATT_HEAD_DIM), q_gain)
    k = rms_norm(k.reshape(B, S, ATT_KV_HEADS, ATT_HEAD_DIM), k_gain)
    v = v.reshape(B, S, ATT_KV_HEADS, ATT_HEAD_DIM)
    cos, sin = axial_rope_tables(S)
    q = apply_rope(q, cos, sin)
    k = apply_rope(k, cos, sin)
    nb = S // Q_BLOCK
    qb = q.reshape(B, nb, Q_BLOCK, ATT_KV_HEADS, ATT_GROUP, ATT_HEAD_DIM).transpose(1, 0, 3, 4, 2, 5)
    kt = k.transpose(0, 2, 1, 3)
    vt = v.transpose(0, 2, 1, 3)
    scale = ATT_HEAD_DIM ** -0.5

    def block(q_blk):
        s = jnp.einsum('bkgqd,bksd->bkgqs', q_blk, kt, preferred_element_type=jnp.float32) * scale
        p = jax.nn.softmax(s, axis=-1)
        return jnp.einsum('bkgqs,bksd->bkgqd', p.astype(vt.dtype), vt)

    o = lax.map(block, qb)
    o = o.transpose(1, 0, 4, 2, 3, 5).reshape(B, S, ATT_Q_COLS)
    return o @ w_out


def mlstm_scan(q, k, v, log_i, log_f):
    B, H, S, Dk = q.shape
    Dv = v.shape[-1]
    L = ML_CHUNK
    nc = S // L

    def chunks(a):
        return jnp.moveaxis(a.reshape(B, H, nc, L, *a.shape[3:]), 2, 0)

    bcum = jnp.moveaxis(jnp.cumsum(log_f.reshape(B, H, nc, L), axis=-1), 2, 0)
    tril = jnp.tril(jnp.ones((L, L), dtype=bool))

    def step(carry, xs):
        C, n, m = carry
        qc, kc, vc, ic, bc = xs
        dmat = jnp.where(tril, bc[..., :, None] - bc[..., None, :] + ic[..., None, :], NEG)
        inter = bc + m[..., None]
        m_row = jnp.maximum(jnp.max(dmat, axis=-1), inter)
        s = jnp.einsum('bhjd,bhld->bhjl', qc, kc) * jnp.exp(dmat - m_row[..., None])
        decay = jnp.exp(inter - m_row)
        num = jnp.einsum('bhjl,bhlv->bhjv', s, vc) + decay[..., None] * jnp.einsum('bhjd,bhdv->bhjv', qc, C)
        den = jnp.sum(s, axis=-1) + decay * jnp.einsum('bhjd,bhd->bhj', qc, n)
        h = num / jnp.maximum(jnp.abs(den), jnp.exp(-m_row))[..., None]
        g_last = bc[..., -1]
        w = g_last[..., None] - bc + ic
        m_new = jnp.maximum(g_last + m, jnp.max(w, axis=-1))
        carry_decay = jnp.exp(g_last + m - m_new)
        wk = jnp.exp(w - m_new[..., None])[..., None] * kc
        C_new = carry_decay[..., None, None] * C + jnp.einsum('bhlk,bhlv->bhkv', wk, vc)
        n_new = carry_decay[..., None] * n + jnp.sum(wk, axis=2)
        return (C_new, n_new, m_new), h

    init = (jnp.zeros((B, H, Dk, Dv), jnp.float32),
            jnp.zeros((B, H, Dk), jnp.float32),
            jnp.full((B, H), NEG, jnp.float32))
    _, h = lax.scan(step, init, (chunks(q), chunks(k), chunks(v), chunks(log_i), bcum))
    return jnp.moveaxis(h, 0, 2).reshape(B, H, S, Dv)


def mlstm_mixer(xn, w_in, gate_bias, h_gain, w_out):
    B, S, _ = xn.shape
    proj = xn @ w_in
    cuts = [ML_QK_COLS, 2 * ML_QK_COLS, 2 * ML_QK_COLS + ML_V_COLS, 2 * ML_QK_COLS + 2 * ML_V_COLS]
    q, k, v, o, g = jnp.split(proj, cuts, axis=-1)

    def heads(t, d):
        return t.reshape(B, S, ML_HEADS, d).transpose(0, 2, 1, 3).astype(jnp.float32)

    q = heads(q, ML_QK_DIM) * (ML_QK_DIM ** -0.5)
    k = heads(k, ML_QK_DIM)
    v = heads(v, ML_V_DIM)
    g = g.astype(jnp.float32).reshape(B, S, 4, ML_HEADS) + gate_bias.astype(jnp.float32).reshape(4, ML_HEADS)
    g = GATE_CAP * jnp.tanh(g / GATE_CAP)
    g = g.transpose(2, 0, 3, 1)
    h_fwd = mlstm_scan(q, k, v, g[0], jax.nn.log_sigmoid(g[1]))
    flip = lambda a: jnp.flip(a, axis=2)
    h_bwd = flip(mlstm_scan(flip(q), flip(k), flip(v), flip(g[2]), flip(jax.nn.log_sigmoid(g[3]))))
    h = (h_fwd + h_bwd).transpose(0, 2, 1, 3)
    h = rms_norm(h, h_gain.reshape(ML_HEADS, ML_V_DIM)).reshape(B, S, ML_V_COLS)
    h = h * jax.nn.sigmoid(o.astype(jnp.float32))
    return h.astype(xn.dtype) @ w_out


def swiglu(x, w13, w2):
    gate, up = jnp.split(x @ w13, 2, axis=-1)
    return (jax.nn.silu(gate) * up) @ w2


def moe_swiglu(xn, router, w13, w2):
    B, S, D = xn.shape
    t = xn.reshape(B * S, D)
    logits = (t @ router).astype(jnp.float32)
    top_logit, top_idx = lax.top_k(logits, TOP_K)
    top_w = jax.nn.softmax(top_logit, axis=-1)
    gates = jnp.sum(jax.nn.one_hot(top_idx, N_EXPERTS, dtype=jnp.float32) * top_w[..., None], axis=1)
    out = jnp.zeros_like(t)
    for e in range(N_EXPERTS):
        out = out + gates[:, e:e + 1].astype(t.dtype) * swiglu(t, w13[e], w2[e])
    return out.reshape(B, S, D)


def setup_inputs(seed: int = 0) -> dict:
    key = jax.random.key(seed)
    ks = jax.random.split(key, 20)
    f32 = jnp.float32
    n_even = (DEPTH + 1) // 2
    n_odd = DEPTH // 2

    def normal(k, shape, scale):
        return jax.random.normal(k, shape, f32) * scale

    d_inv = D_MODEL ** -0.5
    f_offs = jnp.linspace(3.0, 6.0, ML_HEADS, dtype=f32)
    zeros_h = jnp.zeros((ML_HEADS,), f32)
    gate_offs = jnp.stack([zeros_h, f_offs, zeros_h, f_offs]).reshape(4 * ML_HEADS)
    return {
        "x": normal(ks[0], (BATCH, SEQ, D_MODEL), 1.0),
        "norm_mix": 1.0 + normal(ks[1], (DEPTH, D_MODEL), 0.1),
        "norm_ffn": 1.0 + normal(ks[2], (DEPTH, D_MODEL), 0.1),
        "att_w_in": normal(ks[3], (n_even, D_MODEL, ATT_IN_COLS), d_inv),
        "att_q_gain": 1.0 + normal(ks[4], (n_even, ATT_HEAD_DIM), 0.1),
        "att_k_gain": 1.0 + normal(ks[5], (n_even, ATT_HEAD_DIM), 0.1),
        "att_w_out": normal(ks[6], (n_even, ATT_Q_COLS, D_MODEL), ATT_Q_COLS ** -0.5),
        "ffn_w13": normal(ks[7], (n_even, D_MODEL, 2 * D_FF_DENSE), d_inv),
        "ffn_w2": normal(ks[8], (n_even, D_FF_DENSE, D_MODEL), D_FF_DENSE ** -0.5),
        "ml_w_in": normal(ks[9], (n_odd, D_MODEL, ML_IN_COLS), d_inv),
        "ml_gate_bias": gate_offs[None, :] + normal(ks[10], (n_odd, ML_GATE_COLS), 0.1),
        "ml_h_gain": 1.0 + normal(ks[11], (n_odd, ML_V_COLS), 0.1),
        "ml_w_out": normal(ks[12], (n_odd, ML_V_COLS, D_MODEL), ML_V_COLS ** -0.5),
        "moe_router": normal(ks[13], (n_odd, D_MODEL, N_EXPERTS), d_inv),
        "moe_w13": normal(ks[14], (n_odd, N_EXPERTS, D_MODEL, 2 * D_FF_EXPERT), d_inv),
        "moe_w2": normal(ks[15], (n_odd, N_EXPERTS, D_FF_EXPERT, D_MODEL), D_FF_EXPERT ** -0.5),
    }


def reference(x, norm_mix, norm_ffn, att_w_in, att_q_gain, att_k_gain, att_w_out, ffn_w13, ffn_w2,
              ml_w_in, ml_gate_bias, ml_h_gain, ml_w_out, moe_router, moe_w13, moe_w2):
    for i in range(DEPTH):
        j = i // N_MIXERS
        h = rms_norm(x, norm_mix[i])
        if i % N_MIXERS == 0:
            x = x + attention_mixer(h, att_w_in[j], att_q_gain[j], att_k_gain[j], att_w_out[j])
        else:
            x = x + mlstm_mixer(h, ml_w_in[j], ml_gate_bias[j], ml_h_gain[j], ml_w_out[j])
        h = rms_norm(x, norm_ffn[i])
        if i % 2 == 0:
            x = x + swiglu(h, ffn_w13[j], ffn_w2[j])
        else:
            x = x + moe_swiglu(h, moe_router[j], moe_w13[j], moe_w2[j])
    return x
```

```python
import functools

import jax
import jax.numpy as jnp
from jax import lax
from jax.experimental import pallas as pl
from jax.experimental.pallas import tpu as pltpu

F32 = jnp.float32
BF16 = jnp.bfloat16

EPS = 1e-6
NEG = -1e30

GRID_W = 64
ROPE_THETA = 10000.0
ATT_HEADS = 32
ATT_KV_HEADS = 8
ATT_HEAD_DIM = 128
ATT_GROUP = ATT_HEADS // ATT_KV_HEADS

ML_HEADS = 8
ML_QK_DIM = 256
ML_V_DIM = 512
GATE_CAP = 15.0
ML_CHUNK = 256

N_EXPERTS = 8
MOE_BM = 256

VMEM_LIMIT_BYTES = 56 * 1024 * 1024


def _params(*sem):
    return pltpu.CompilerParams(dimension_semantics=sem, vmem_limit_bytes=VMEM_LIMIT_BYTES)


def _rmsnorm_body(x_ref, g_ref, o_ref):
    x = x_ref[...]
    inv = lax.rsqrt(jnp.mean(x * x, axis=-1, keepdims=True) + EPS)
    o_ref[...] = (x * inv * g_ref[...]).astype(o_ref.dtype)


def rmsnorm(x, gain, out_dtype, bm=256):
    t, d = x.shape
    return pl.pallas_call(
        _rmsnorm_body,
        grid=(t // bm,),
        in_specs=[pl.BlockSpec((bm, d), lambda i: (i, 0)), pl.BlockSpec((1, d), lambda i: (0, 0))],
        out_specs=pl.BlockSpec((bm, d), lambda i: (i, 0)),
        out_shape=jax.ShapeDtypeStruct((t, d), out_dtype),
        compiler_params=_params("parallel"),
        name="rmsnorm",
    )(x, gain.reshape(1, d))


def _mm_body(a_ref, w_ref, o_ref):
    o_ref[...] = jnp.dot(a_ref[...], w_ref[...], preferred_element_type=F32).astype(o_ref.dtype)


def _mm_res_body(a_ref, w_ref, r_ref, o_ref):
    o_ref[...] = r_ref[...] + jnp.dot(a_ref[...], w_ref[...], preferred_element_type=F32)


def matmul(a, w, n, out_dtype, bm, bn, residual=None):
    m, k = a.shape
    grid = (n // bn, m // bm)
    in_specs = [pl.BlockSpec((bm, k), lambda j, i: (i, 0)), pl.BlockSpec((k, bn), lambda j, i: (0, j))]
    args = [a, w]
    body = _mm_body
    if residual is not None:
        in_specs.append(pl.BlockSpec((bm, bn), lambda j, i: (i, j)))
        args.append(residual)
        body = _mm_res_body
    return pl.pallas_call(
        body,
        grid=grid,
        in_specs=in_specs,
        out_specs=pl.BlockSpec((bm, bn), lambda j, i: (i, j)),
        out_shape=jax.ShapeDtypeStruct((m, n), out_dtype),
        compiler_params=_params("parallel", "parallel"),
        name="matmul",
    )(*args)


def _mm_kgrid_res_body(a_ref, w_ref, r_ref, o_ref):
    part = jnp.dot(a_ref[...], w_ref[...], preferred_element_type=F32)

    @pl.when(pl.program_id(2) == 0)
    def _():
        o_ref[...] = r_ref[...] + part

    @pl.when(pl.program_id(2) != 0)
    def _():
        o_ref[...] += part


def matmul_kgrid_res(a, w, residual, bm, bn, bk):
    m, k = a.shape
    n = w.shape[1]
    return pl.pallas_call(
        _mm_kgrid_res_body,
        grid=(n // bn, m // bm, k // bk),
        in_specs=[
            pl.BlockSpec((bm, bk), lambda j, i, kk: (i, kk)),
            pl.BlockSpec((bk, bn), lambda j, i, kk: (kk, j)),
            pl.BlockSpec((bm, bn), lambda j, i, kk: (i, j)),
        ],
        out_specs=pl.BlockSpec((bm, bn), lambda j, i, kk: (i, j)),
        out_shape=jax.ShapeDtypeStruct((m, n), F32),
        compiler_params=_params("parallel", "parallel", "arbitrary"),
        name="matmul_kgrid_res",
    )(a, w, residual)


def _silu_mul(g, u):
    return g / (1.0 + jnp.exp(-g)) * u


def _mm_swiglu_body(a_ref, wg_ref, wu_ref, o_ref):
    a = a_ref[...]
    g = jnp.dot(a, wg_ref[...], preferred_element_type=F32)
    u = jnp.dot(a, wu_ref[...], preferred_element_type=F32)
    o_ref[...] = _silu_mul(g, u).astype(o_ref.dtype)


def matmul_swiglu(a, w13, bm, bn):
    m, k = a.shape
    f = w13.shape[1] // 2
    nb = f // bn
    return pl.pallas_call(
        _mm_swiglu_body,
        grid=(nb, m // bm),
        in_specs=[
            pl.BlockSpec((bm, k), lambda j, i: (i, 0)),
            pl.BlockSpec((k, bn), lambda j, i: (0, j)),
            pl.BlockSpec((k, bn), lambda j, i: (0, j + nb)),
        ],
        out_specs=pl.BlockSpec((bm, bn), lambda j, i: (i, j)),
        out_shape=jax.ShapeDtypeStruct((m, f), BF16),
        compiler_params=_params("parallel", "parallel"),
        name="matmul_swiglu",
    )(a, w13, w13)


def _split_bf16(x):
    hi = x.astype(BF16)
    lo = (x - hi.astype(F32)).astype(BF16)
    return hi, lo


def _norm_mm_hp_body(x_ref, g_ref, w_ref, o_ref):
    x = x_ref[...]
    inv = lax.rsqrt(jnp.mean(x * x, axis=-1, keepdims=True) + EPS)
    xn = x * inv * g_ref[...]
    xh, xl = _split_bf16(xn)
    wh, wl = _split_bf16(w_ref[...])
    acc = jnp.dot(xh, wh, preferred_element_type=F32)
    acc += jnp.dot(xh, wl, preferred_element_type=F32)
    acc += jnp.dot(xl, wh, preferred_element_type=F32)
    o_ref[...] = acc


def norm_matmul_hp(x, gain, w, bm=512):
    t, d = x.shape
    n = w.shape[1]
    return pl.pallas_call(
        _norm_mm_hp_body,
        grid=(t // bm,),
        in_specs=[
            pl.BlockSpec((bm, d), lambda i: (i, 0)),
            pl.BlockSpec((1, d), lambda i: (0, 0)),
            pl.BlockSpec((d, n), lambda i: (0, 0)),
        ],
        out_specs=pl.BlockSpec((bm, n), lambda i: (i, 0)),
        out_shape=jax.ShapeDtypeStruct((t, n), F32),
        compiler_params=_params("parallel"),
        name="norm_matmul_hp",
    )(x, gain.reshape(1, d), w)


def _qk_rope_body(p_ref, cos_ref, sin_ref, qg_ref, kg_ref, q_ref, k_ref, v_ref, *, n_q, n_kv, dh, q_scale):
    cos = cos_ref[...]
    sin = sin_ref[...]
    even = (lax.broadcasted_iota(jnp.int32, cos.shape, 1) % 2) == 0

    def norm_rope(x, gain):
        xn = x * lax.rsqrt(jnp.mean(x * x, axis=-1, keepdims=True) + EPS) * gain
        partner = jnp.where(even, pltpu.roll(xn, dh - 1, 1), pltpu.roll(xn, 1, 1))
        return xn * cos + partner * sin

    for h in range(n_q):
        x = p_ref[:, h * dh:(h + 1) * dh]
        q_ref[:, h * dh:(h + 1) * dh] = (norm_rope(x, qg_ref[...]) * q_scale).astype(q_ref.dtype)
    for h in range(n_kv):
        x = p_ref[:, (n_q + h) * dh:(n_q + h + 1) * dh]
        k_ref[:, h * dh:(h + 1) * dh] = norm_rope(x, kg_ref[...]).astype(k_ref.dtype)
    v0 = (n_q + n_kv) * dh
    v_ref[...] = p_ref[:, v0:v0 + n_kv * dh].astype(v_ref.dtype)


def qk_norm_rope(proj, cos_full, sin_signed, q_gain, k_gain, seq, n_q, n_kv, dh, bm=256):
    t = proj.shape[0]
    sb = seq // bm
    body = functools.partial(_qk_rope_body, n_q=n_q, n_kv=n_kv, dh=dh, q_scale=dh ** -0.5)
    return pl.pallas_call(
        body,
        grid=(t // bm,),
        in_specs=[
            pl.BlockSpec((bm, proj.shape[1]), lambda i: (i, 0)),
            pl.BlockSpec((bm, dh), lambda i: (i % sb, 0)),
            pl.BlockSpec((bm, dh), lambda i: (i % sb, 0)),
            pl.BlockSpec((1, dh), lambda i: (0, 0)),
            pl.BlockSpec((1, dh), lambda i: (0, 0)),
        ],
        out_specs=[
            pl.BlockSpec((bm, n_q * dh), lambda i: (i, 0)),
            pl.BlockSpec((bm, n_kv * dh), lambda i: (i, 0)),
            pl.BlockSpec((bm, n_kv * dh), lambda i: (i, 0)),
        ],
        out_shape=[
            jax.ShapeDtypeStruct((t, n_q * dh), BF16),
            jax.ShapeDtypeStruct((t, n_kv * dh), BF16),
            jax.ShapeDtypeStruct((t, n_kv * dh), BF16),
        ],
        compiler_params=_params("parallel"),
        name="qk_norm_rope",
    )(proj, cos_full, sin_signed, q_gain.reshape(1, dh), k_gain.reshape(1, dh))


def _attn_body(q_ref, k_ref, v_ref, o_ref, *, group, dh):
    k = k_ref[...]
    v = v_ref[...]
    for g in range(group):
        q = q_ref[:, g * dh:(g + 1) * dh]
        s = lax.dot_general(q, k, (((1,), (1,)), ((), ())), preferred_element_type=F32)
        m = jnp.max(s, axis=-1, keepdims=True)
        p = jnp.exp(s - m)
        l = jnp.sum(p, axis=-1, keepdims=True)
        o = jnp.dot(p.astype(BF16), v, preferred_element_type=F32)
        o_ref[:, g * dh:(g + 1) * dh] = (o * (1.0 / l)).astype(o_ref.dtype)


def attention(q, k, v, n_kv, group, dh, tq=256):
    b, s, _ = q.shape
    body = functools.partial(_attn_body, group=group, dh=dh)
    return pl.pallas_call(
        body,
        grid=(b, n_kv, s // tq),
        in_specs=[
            pl.BlockSpec((None, tq, group * dh), lambda bi, kv, i: (bi, i, kv)),
            pl.BlockSpec((None, s, dh), lambda bi, kv, i: (bi, 0, kv)),
            pl.BlockSpec((None, s, dh), lambda bi, kv, i: (bi, 0, kv)),
        ],
        out_specs=pl.BlockSpec((None, tq, group * dh), lambda bi, kv, i: (bi, i, kv)),
        out_shape=jax.ShapeDtypeStruct(q.shape, BF16),
        compiler_params=_params("parallel", "parallel", "parallel"),
        name="attention",
    )(q, k, v)


def _cumsum_dot(x, tri):
    hi = x.astype(BF16)
    r1 = x - hi.astype(F32)
    mid = r1.astype(BF16)
    lo = (r1 - mid.astype(F32)).astype(BF16)
    acc = jnp.dot(hi, tri, preferred_element_type=F32)
    acc += jnp.dot(mid, tri, preferred_element_type=F32)
    acc += jnp.dot(lo, tri, preferred_element_type=F32)
    return acc


def _gates_body(pre_ref, bias_ref, o_ref, *, heads, chunk_shift):
    g = pre_ref[...] + bias_ref[...]
    g = GATE_CAP * jnp.tanh(g / GATE_CAP)
    logsig = jnp.minimum(g, 0.0) - jnp.log1p(jnp.exp(-jnp.abs(g)))
    n = g.shape[1]
    t = lax.broadcasted_iota(jnp.int32, (n, n), 0)
    j = lax.broadcasted_iota(jnp.int32, (n, n), 1)
    same = lax.shift_right_logical(t, chunk_shift) == lax.shift_right_logical(j, chunk_shift)
    tri_f = jnp.where(jnp.logical_and(same, t <= j), 1.0, 0.0).astype(BF16)
    tri_b = jnp.where(jnp.logical_and(same, t >= j), 1.0, 0.0).astype(BF16)
    h = heads
    bc_f = _cumsum_dot(logsig[h:2 * h], tri_f)
    suf_b = _cumsum_dot(logsig[3 * h:4 * h], tri_b)
    o_ref[0:h, :] = bc_f
    o_ref[h:2 * h, :] = g[0:h] - bc_f
    o_ref[2 * h:3 * h, :] = suf_b
    o_ref[3 * h:4 * h, :] = g[2 * h:3 * h] - suf_b


def mlstm_gates(pre_rows, bias, heads, chunk, sb=512):
    b, r, s = pre_rows.shape
    body = functools.partial(_gates_body, heads=heads, chunk_shift=chunk.bit_length() - 1)
    return pl.pallas_call(
        body,
        grid=(b, s // sb),
        in_specs=[
            pl.BlockSpec((None, r, sb), lambda bi, i: (bi, 0, i)),
            pl.BlockSpec((r, 1), lambda bi, i: (0, 0)),
        ],
        out_specs=pl.BlockSpec((None, r, sb), lambda bi, i: (bi, 0, i)),
        out_shape=jax.ShapeDtypeStruct((b, r, s), F32),
        compiler_params=_params("parallel", "parallel"),
        name="mlstm_gates",
    )(pre_rows, bias.reshape(r, 1))


def _mlstm_body(q_ref, k_ref, v_ref, gcol_ref, grow_ref, h_ref, c_ref, n_ref, m_ref, *, heads, q_scale):
    d = pl.program_id(1)
    hd = pl.program_id(2)

    @pl.when(pl.program_id(3) == 0)
    def _():
        c_ref[...] = jnp.zeros(c_ref.shape, F32)
        n_ref[...] = jnp.zeros(n_ref.shape, F32)
        m_ref[...] = jnp.full(m_ref.shape, NEG, F32)

    q = q_ref[...]
    k = k_ref[...]
    v = v_ref[...]
    ln = q.shape[0]
    gcol = gcol_ref[...]
    lane = lax.broadcasted_iota(jnp.int32, gcol.shape, 1)
    base = 2 * heads * d + hd
    bcol = jnp.sum(jnp.where(lane == base, gcol, 0.0), axis=1, keepdims=True)
    rcol = jnp.sum(jnp.where(lane == base + heads, gcol, 0.0), axis=1, keepdims=True)
    rrow = grow_ref[pl.ds(base + heads, 1), :]

    row_i = lax.broadcasted_iota(jnp.int32, (ln, ln), 0)
    col_i = lax.broadcasted_iota(jnp.int32, (ln, ln), 1)
    sign = 1 - 2 * d
    mask = (col_i - row_i) * sign <= 0
    dmat = jnp.where(mask, bcol + rrow, NEG)
    m_prev = m_ref[...]
    inter = bcol + m_prev
    m_row = jnp.maximum(jnp.max(dmat, axis=1, keepdims=True), inter)
    qk = lax.dot_general(q, k, (((1,), (1,)), ((), ())), preferred_element_type=F32) * q_scale
    s = qk * jnp.exp(dmat - m_row)
    decay = jnp.exp(inter - m_row)
    q_c = jnp.dot(q, c_ref[...].astype(BF16), preferred_element_type=F32) * q_scale
    num = jnp.dot(s.astype(BF16), v, preferred_element_type=F32) + decay * q_c
    q_n = jnp.sum(q.astype(F32) * n_ref[...], axis=1, keepdims=True) * q_scale
    den = jnp.sum(s, axis=1, keepdims=True) + decay * q_n
    h_ref[...] = num / jnp.maximum(jnp.abs(den), jnp.exp(-m_row))

    g_last = jnp.where(d == 0, bcol[ln - 1:ln, :], bcol[0:1, :])
    wcol = g_last + rcol
    m_new = jnp.maximum(g_last + m_prev, jnp.max(wcol, axis=0, keepdims=True))
    carry_decay = jnp.exp(g_last + m_prev - m_new)
    wk = jnp.exp(wcol - m_new) * k.astype(F32)
    kv = lax.dot_general(wk.astype(BF16), v, (((0,), (0,)), ((), ())), preferred_element_type=F32)
    c_ref[...] = carry_decay * c_ref[...] + kv
    n_ref[...] = carry_decay * n_ref[...] + jnp.sum(wk, axis=0, keepdims=True)
    m_ref[...] = m_new


def mlstm_scan(proj, gcol, grow, heads, dk, dv, chunk):
    b, s, _ = proj.shape
    nc = s // chunk
    body = functools.partial(_mlstm_body, heads=heads, q_scale=dk ** -0.5)

    def cidx(d, c):
        return c + d * (nc - 1 - 2 * c)

    k_blk0 = heads
    v_blk0 = (2 * heads * dk) // dv
    return pl.pallas_call(
        body,
        grid=(b, 2, heads, nc),
        in_specs=[
            pl.BlockSpec((None, chunk, dk), lambda bi, d, h, c: (bi, cidx(d, c), h)),
            pl.BlockSpec((None, chunk, dk), lambda bi, d, h, c: (bi, cidx(d, c), k_blk0 + h)),
            pl.BlockSpec((None, chunk, dv), lambda bi, d, h, c: (bi, cidx(d, c), v_blk0 + h)),
            pl.BlockSpec((None, chunk, 4 * heads), lambda bi, d, h, c: (bi, cidx(d, c), 0)),
            pl.BlockSpec((None, 4 * heads, chunk), lambda bi, d, h, c: (bi, 0, cidx(d, c))),
        ],
        out_specs=pl.BlockSpec((None, None, chunk, dv), lambda bi, d, h, c: (d, bi, cidx(d, c), h)),
        out_shape=jax.ShapeDtypeStruct((2, b, s, heads * dv), F32),
        scratch_shapes=[
            pltpu.VMEM((dk, dv), F32),
            pltpu.VMEM((1, dk), F32),
            pltpu.VMEM((1, 1), F32),
        ],
        compiler_params=_params("parallel", "parallel", "parallel", "arbitrary"),
        name="mlstm_scan",
    )(proj, proj, proj, gcol, grow)


def _ml_out_body(hf_ref, hb_ref, o_ref, g_ref, out_ref, *, heads, dv):
    for h in range(heads):
        sl = slice(h * dv, (h + 1) * dv)
        x = hf_ref[:, sl] + hb_ref[:, sl]
        xn = x * lax.rsqrt(jnp.mean(x * x, axis=-1, keepdims=True) + EPS) * g_ref[:, sl]
        og = o_ref[:, sl].astype(F32)
        out_ref[:, sl] = (xn / (1.0 + jnp.exp(-og))).astype(out_ref.dtype)


def mlstm_out_gate(h_dirs, proj, h_gain, heads, dv, o_blk, bm=256):
    _, t, d = h_dirs.shape
    body = functools.partial(_ml_out_body, heads=heads, dv=dv)
    return pl.pallas_call(
        body,
        grid=(t // bm,),
        in_specs=[
            pl.BlockSpec((None, bm, d), lambda i: (0, i, 0)),
            pl.BlockSpec((None, bm, d), lambda i: (1, i, 0)),
            pl.BlockSpec((bm, d), lambda i: (i, o_blk)),
            pl.BlockSpec((1, d), lambda i: (0, 0)),
        ],
        out_specs=pl.BlockSpec((bm, d), lambda i: (i, 0)),
        out_shape=jax.ShapeDtypeStruct((t, d), BF16),
        compiler_params=_params("parallel"),
        name="mlstm_out_gate",
    )(h_dirs, h_dirs, proj, h_gain.reshape(1, d))


def _route_body(lg_ref, idx_ref, rank_ref, w_ref, cnt_ref, carry_ref):
    @pl.when(pl.program_id(0) == 0)
    def _():
        carry_ref[...] = jnp.zeros(carry_ref.shape, F32)

    lg = lg_ref[...]
    n_e, tb = lg.shape
    e_iota = lax.broadcasted_iota(jnp.int32, lg.shape, 0).astype(F32)
    t1 = jnp.max(lg, axis=0, keepdims=True)
    i1 = jnp.min(jnp.where(lg == t1, e_iota, float(n_e)), axis=0, keepdims=True)
    first = e_iota == i1
    lg2 = jnp.where(first, -jnp.inf, lg)
    t2 = jnp.max(lg2, axis=0, keepdims=True)
    i2 = jnp.min(jnp.where(lg2 == t2, e_iota, float(n_e)), axis=0, keepdims=True)
    second = e_iota == i2
    e2 = jnp.exp(t2 - t1)
    w_ref[0:1, :] = 1.0 / (1.0 + e2)
    w_ref[1:2, :] = e2 / (1.0 + e2)
    idx_ref[0:1, :] = i1.astype(jnp.int32)
    idx_ref[1:2, :] = i2.astype(jnp.int32)

    assign = jnp.where(first, 1.0, 0.0) + jnp.where(second, 1.0, 0.0)
    tp = lax.broadcasted_iota(jnp.int32, (tb, tb), 0)
    tc = lax.broadcasted_iota(jnp.int32, (tb, tb), 1)
    before = jnp.where(tp < tc, 1.0, 0.0).astype(BF16)
    rank = jnp.dot(assign.astype(BF16), before, preferred_element_type=F32) + carry_ref[:, 0:1]
    rank_ref[0:1, :] = jnp.sum(jnp.where(first, rank, 0.0), axis=0, keepdims=True).astype(jnp.int32)
    rank_ref[1:2, :] = jnp.sum(jnp.where(second, rank, 0.0), axis=0, keepdims=True).astype(jnp.int32)
    carry_ref[...] = carry_ref[...] + jnp.sum(assign, axis=1, keepdims=True)
    cnt_ref[...] = carry_ref[...]


def route_top2(logits_rows, tb=512):
    n_e, t = logits_rows.shape
    return pl.pallas_call(
        _route_body,
        grid=(t // tb,),
        in_specs=[pl.BlockSpec((n_e, tb), lambda i: (0, i))],
        out_specs=[
            pl.BlockSpec((2, tb), lambda i: (0, i)),
            pl.BlockSpec((2, tb), lambda i: (0, i)),
            pl.BlockSpec((2, tb), lambda i: (0, i)),
            pl.BlockSpec((n_e, 128), lambda i: (0, 0)),
        ],
        out_shape=[
            jax.ShapeDtypeStruct((2, t), jnp.int32),
            jax.ShapeDtypeStruct((2, t), jnp.int32),
            jax.ShapeDtypeStruct((2, t), F32),
            jax.ShapeDtypeStruct((n_e, 128), F32),
        ],
        scratch_shapes=[pltpu.VMEM((n_e, 128), F32)],
        compiler_params=_params("arbitrary"),
        name="route_top2",
    )(logits_rows)


def _dispatch_body(pos_ref, x_hbm, buf_in, buf_hbm, sem, *, tb, t_total):
    del buf_in
    t0 = pl.program_id(0) * tb

    def row_copy(src_row, dst_row):
        return pltpu.make_async_copy(x_hbm.at[pl.ds(src_row, 1)], buf_hbm.at[pl.ds(dst_row, 1)], sem)

    def issue(j, carry):
        t = t0 + j
        row_copy(t, pos_ref[t]).start()
        row_copy(t, pos_ref[t_total + t]).start()
        return carry

    lax.fori_loop(0, tb, issue, 0)

    def drain(j, carry):
        row_copy(0, 0).wait()
        row_copy(0, 0).wait()
        return carry

    lax.fori_loop(0, tb, drain, 0)


def moe_dispatch(x, pos_flat, n_rows, tb=256):
    t, d = x.shape
    body = functools.partial(_dispatch_body, tb=tb, t_total=t)
    return pl.pallas_call(
        body,
        grid_spec=pltpu.PrefetchScalarGridSpec(
            num_scalar_prefetch=1,
            grid=(t // tb,),
            in_specs=[pl.BlockSpec(memory_space=pl.ANY), pl.BlockSpec(memory_space=pl.ANY)],
            out_specs=pl.BlockSpec(memory_space=pl.ANY),
            scratch_shapes=[pltpu.SemaphoreType.DMA(())],
        ),
        out_shape=jax.ShapeDtypeStruct((n_rows, d), x.dtype),
        input_output_aliases={2: 0},
        compiler_params=_params("arbitrary"),
        name="moe_dispatch",
    )(pos_flat, x, jnp.zeros((n_rows, d), x.dtype))


def _gmm_swiglu_body(src_ref, exp_ref, used_ref, x_ref, wg_ref, wu_ref, o_ref):
    del src_ref, exp_ref

    @pl.when(pl.program_id(1) < used_ref[0])
    def _():
        a = x_ref[...].astype(BF16)
        g = jnp.dot(a, wg_ref[...], preferred_element_type=F32)
        u = jnp.dot(a, wu_ref[...], preferred_element_type=F32)
        o_ref[...] = _silu_mul(g, u).astype(o_ref.dtype)

    @pl.when(pl.program_id(1) >= used_ref[0])
    def _():
        o_ref[...] = jnp.zeros(o_ref.shape, o_ref.dtype)


def grouped_swiglu(xs, w13, tile_src, tile_exp, n_used, bm, bn):
    r, d = xs.shape
    f = w13.shape[2] // 2
    nb = f // bn
    return pl.pallas_call(
        _gmm_swiglu_body,
        grid_spec=pltpu.PrefetchScalarGridSpec(
            num_scalar_prefetch=3,
            grid=(nb, r // bm),
            in_specs=[
                pl.BlockSpec((bm, d), lambda j, i, src, ex, nu: (src[i], 0)),
                pl.BlockSpec((None, d, bn), lambda j, i, src, ex, nu: (ex[i], 0, j)),
                pl.BlockSpec((None, d, bn), lambda j, i, src, ex, nu: (ex[i], 0, j + nb)),
            ],
            out_specs=pl.BlockSpec((bm, bn), lambda j, i, src, ex, nu: (i, j)),
        ),
        out_shape=jax.ShapeDtypeStruct((r, f), BF16),
        compiler_params=_params("arbitrary", "arbitrary"),
        name="grouped_swiglu",
    )(tile_src, tile_exp, n_used, xs, w13, w13)


def _gmm_body(src_ref, exp_ref, used_ref, a_ref, w_ref, o_ref):
    del src_ref, exp_ref

    @pl.when(pl.program_id(1) < used_ref[0])
    def _():
        o_ref[...] = jnp.dot(a_ref[...], w_ref[...], preferred_element_type=F32).astype(o_ref.dtype)

    @pl.when(pl.program_id(1) >= used_ref[0])
    def _():
        o_ref[...] = jnp.zeros(o_ref.shape, o_ref.dtype)


def grouped_matmul(a, w, tile_src, tile_exp, n_used, bm, bn):
    r, k = a.shape
    n = w.shape[2]
    return pl.pallas_call(
        _gmm_body,
        grid_spec=pltpu.PrefetchScalarGridSpec(
            num_scalar_prefetch=3,
            grid=(n // bn, r // bm),
            in_specs=[
                pl.BlockSpec((bm, k), lambda j, i, src, ex, nu: (src[i], 0)),
                pl.BlockSpec((None, k, bn), lambda j, i, src, ex, nu: (ex[i], 0, j)),
            ],
            out_specs=pl.BlockSpec((bm, bn), lambda j, i, src, ex, nu: (i, j)),
        ),
        out_shape=jax.ShapeDtypeStruct((r, n), F32),
        compiler_params=_params("arbitrary", "arbitrary"),
        name="grouped_matmul",
    )(tile_src, tile_exp, n_used, a, w)


def _combine_body(pos_ref, x_ref, w_ref, y_hbm, o_ref, ybuf, sem, *, tb, t_total):
    t0 = pl.program_id(0) * tb

    def row_copy(src_row, choice, j):
        return pltpu.make_async_copy(y_hbm.at[pl.ds(src_row, 1)], ybuf.at[choice, pl.ds(j, 1)], sem)

    def issue(j, carry):
        t = t0 + j
        row_copy(pos_ref[t], 0, j).start()
        row_copy(pos_ref[t_total + t], 1, j).start()
        return carry

    lax.fori_loop(0, tb, issue, 0)

    def drain(j, carry):
        row_copy(0, 0, 0).wait()
        row_copy(0, 1, 0).wait()
        return carry

    lax.fori_loop(0, tb, drain, 0)
    w = w_ref[...]
    o_ref[...] = x_ref[...] + w[:, 0:1] * ybuf[0] + w[:, 1:2] * ybuf[1]


def moe_combine(x, y, pos_flat, w_cols, tb=256):
    t, d = x.shape
    body = functools.partial(_combine_body, tb=tb, t_total=t)
    return pl.pallas_call(
        body,
        grid_spec=pltpu.PrefetchScalarGridSpec(
            num_scalar_prefetch=1,
            grid=(t // tb,),
            in_specs=[
                pl.BlockSpec((tb, d), lambda i, pos: (i, 0)),
                pl.BlockSpec((tb, 2), lambda i, pos: (i, 0)),
                pl.BlockSpec(memory_space=pl.ANY),
            ],
            out_specs=pl.BlockSpec((tb, d), lambda i, pos: (i, 0)),
            scratch_shapes=[pltpu.VMEM((2, tb, d), F32), pltpu.SemaphoreType.DMA(())],
        ),
        out_shape=jax.ShapeDtypeStruct((t, d), F32),
        compiler_params=_params("arbitrary"),
        name="moe_combine",
    )(pos_flat, x, w_cols, y)


def moe_plan(idx, rank, counts, bm, max_tiles):
    tiles = (counts + bm - 1) // bm
    tile_end = jnp.cumsum(tiles)
    row_off = (tile_end - tiles) * bm
    pos = row_off[idx] + rank
    n_used = tile_end[-1]
    tile_src = jnp.minimum(jnp.arange(max_tiles, dtype=jnp.int32), n_used - 1)
    tile_exp = jnp.sum(tile_src[:, None] >= tile_end[None, :], axis=1).astype(jnp.int32)
    return pos.reshape(-1).astype(jnp.int32), tile_src, tile_exp, n_used.reshape(1).astype(jnp.int32)


def _rope_tables(seq):
    rows = seq // GRID_W
    row_ids = jnp.repeat(jnp.arange(rows, dtype=F32), GRID_W)
    col_ids = jnp.tile(jnp.arange(GRID_W, dtype=F32), rows)
    n_freq = ATT_HEAD_DIM // 4
    inv_freq = ROPE_THETA ** (-jnp.arange(n_freq, dtype=F32) / n_freq)
    ang = jnp.concatenate([row_ids[:, None] * inv_freq, col_ids[:, None] * inv_freq], axis=-1)
    cos, sin = jnp.cos(ang), jnp.sin(ang)
    cos_full = jnp.repeat(cos, 2, axis=-1)
    sin_signed = jnp.stack([-sin, sin], axis=-1).reshape(seq, ATT_HEAD_DIM)
    return cos_full, sin_signed


def _attention_layer(x, b, s, norm_g, w_in, q_gain, k_gain, w_out):
    t, d = x.shape
    hn = rmsnorm(x, norm_g, BF16)
    n_in = w_in.shape[1]
    proj = matmul(hn, w_in.astype(BF16), n_in, F32, bm=1024, bn=1024)
    cos_full, sin_signed = _rope_tables(s)
    q, k, v = qk_norm_rope(proj, cos_full, sin_signed, q_gain, k_gain, s, ATT_HEADS, ATT_KV_HEADS, ATT_HEAD_DIM)
    o = attention(q.reshape(b, s, -1), k.reshape(b, s, -1), v.reshape(b, s, -1),
                  ATT_KV_HEADS, ATT_GROUP, ATT_HEAD_DIM)
    return matmul(o.reshape(t, -1), w_out.astype(BF16), d, F32, bm=512, bn=1024, residual=x)


def _dense_ffn_layer(x, norm_g, w13, w2):
    hn = rmsnorm(x, norm_g, BF16)
    act = matmul_swiglu(hn, w13.astype(BF16), bm=1024, bn=512)
    return matmul_kgrid_res(act, w2.astype(BF16), x, bm=1024, bn=1024, bk=2048)


def _mlstm_layer(x, b, s, norm_g, w_in, gate_bias, h_gain, w_out):
    t, d = x.shape
    n_main = 2 * ML_HEADS * ML_QK_DIM + 2 * ML_HEADS * ML_V_DIM
    hn = rmsnorm(x, norm_g, BF16)
    proj = matmul(hn, w_in.astype(BF16), n_main, BF16, bm=1024, bn=1024)
    pre = norm_matmul_hp(x, norm_g, w_in[:, n_main:])
    pre_rows = pre.reshape(b, s, 4 * ML_HEADS).transpose(0, 2, 1)
    grow = mlstm_gates(pre_rows, gate_bias, ML_HEADS, ML_CHUNK)
    gcol = grow.transpose(0, 2, 1)
    h_dirs = mlstm_scan(proj.reshape(b, s, n_main), gcol, grow, ML_HEADS, ML_QK_DIM, ML_V_DIM, ML_CHUNK)
    o_blk = (2 * ML_HEADS * ML_QK_DIM + ML_HEADS * ML_V_DIM) // (ML_HEADS * ML_V_DIM)
    gated = mlstm_out_gate(h_dirs.reshape(2, t, -1), proj, h_gain, ML_HEADS, ML_V_DIM, o_blk)
    return matmul(gated, w_out.astype(BF16), d, F32, bm=512, bn=1024, residual=x)


def _moe_layer(x, norm_g, router, w13, w2):
    t, d = x.shape
    hn = rmsnorm(x, norm_g, F32)
    logits = norm_matmul_hp(x, norm_g, router)
    idx, rank, w_rows, cnt = route_top2(logits.T)
    max_tiles = (2 * t) // MOE_BM + N_EXPERTS
    pos, tile_src, tile_exp, n_used = moe_plan(idx, rank, cnt[:, 0].astype(jnp.int32), MOE_BM, max_tiles)
    xs = moe_dispatch(hn, pos, max_tiles * MOE_BM)
    act = grouped_swiglu(xs, w13.astype(BF16), tile_src, tile_exp, n_used, MOE_BM, 512)
    y = grouped_matmul(act, w2.astype(BF16), tile_src, tile_exp, n_used, MOE_BM, 1024)
    return moe_combine(x, y, pos, w_rows.T)


def kernel(x, norm_mix, norm_ffn, att_w_in, att_q_gain, att_k_gain, att_w_out, ffn_w13, ffn_w2,
           ml_w_in, ml_gate_bias, ml_h_gain, ml_w_out, moe_router, moe_w13, moe_w2):
    b, s, d = x.shape
    h = x.reshape(b * s, d)
    h = _attention_layer(h, b, s, norm_mix[0], att_w_in[0], att_q_gain[0], att_k_gain[0], att_w_out[0])
    h = _dense_ffn_layer(h, norm_ffn[0], ffn_w13[0], ffn_w2[0])
    h = _mlstm_layer(h, b, s, norm_mix[1], ml_w_in[0], ml_gate_bias[0], ml_h_gain[0], ml_w_out[0])
    h = _moe_layer(h, norm_ffn[1], moe_router[0], moe_w13[0], moe_w2[0])
    return h.reshape(b, s, d)
```

```python
import functools

import jax
import jax.numpy as jnp
from jax import lax
from jax.experimental import pallas as pl
from jax.experimental.pallas import tpu as pltpu

F32 = jnp.float32
BF16 = jnp.bfloat16

EPS = 1e-6
NEG = -1e30

GRID_W = 64
ROPE_THETA = 10000.0
ATT_HEADS = 32
ATT_KV_HEADS = 8
ATT_HEAD_DIM = 128
ATT_GROUP = ATT_HEADS // ATT_KV_HEADS

ML_HEADS = 8
ML_QK_DIM = 256
ML_V_DIM = 512
GATE_CAP = 15.0
ML_CHUNK = 256

N_EXPERTS = 8
MOE_BM = 256

VMEM_LIMIT_BYTES = 56 * 1024 * 1024


def _params(*sem):
    return pltpu.CompilerParams(dimension_semantics=sem, vmem_limit_bytes=VMEM_LIMIT_BYTES)


def _rmsnorm_body(x_ref, g_ref, o_ref):
    x = x_ref[...]
    inv = lax.rsqrt(jnp.mean(x * x, axis=-1, keepdims=True) + EPS)
    o_ref[...] = (x * inv * g_ref[...]).astype(o_ref.dtype)


def rmsnorm(x, gain, out_dtype, bm=256):
    t, d = x.shape
    return pl.pallas_call(
        _rmsnorm_body,
        grid=(t // bm,),
        in_specs=[pl.BlockSpec((bm, d), lambda i: (i, 0)), pl.BlockSpec((1, d), lambda i: (0, 0))],
        out_specs=pl.BlockSpec((bm, d), lambda i: (i, 0)),
        out_shape=jax.ShapeDtypeStruct((t, d), out_dtype),
        compiler_params=_params("parallel"),
        name="rmsnorm",
    )(x, gain.reshape(1, d))


def _mm_body(a_ref, w_ref, o_ref, wb_ref):
    @pl.when(pl.program_id(1) == 0)
    def _():
        wb_ref[...] = w_ref[...].astype(BF16)

    o_ref[...] = jnp.dot(a_ref[...], wb_ref[...], preferred_element_type=F32).astype(o_ref.dtype)


def _mm_res_body(a_ref, w_ref, r_ref, o_ref, wb_ref):
    @pl.when(pl.program_id(1) == 0)
    def _():
        wb_ref[...] = w_ref[...].astype(BF16)

    o_ref[...] = r_ref[...] + jnp.dot(a_ref[...], wb_ref[...], preferred_element_type=F32)


def matmul(a, w, n, out_dtype, bm, bn, residual=None):
    m, k = a.shape
    in_specs = [pl.BlockSpec((bm, k), lambda j, i: (i, 0)), pl.BlockSpec((k, bn), lambda j, i: (0, j))]
    args = [a, w]
    body = _mm_body
    if residual is not None:
        in_specs.append(pl.BlockSpec((bm, bn), lambda j, i: (i, j)))
        args.append(residual)
        body = _mm_res_body
    return pl.pallas_call(
        body,
        grid=(n // bn, m // bm),
        in_specs=in_specs,
        out_specs=pl.BlockSpec((bm, bn), lambda j, i: (i, j)),
        out_shape=jax.ShapeDtypeStruct((m, n), out_dtype),
        scratch_shapes=[pltpu.VMEM((k, bn), BF16)],
        compiler_params=_params("parallel", "arbitrary"),
        name="matmul",
    )(*args)


def _mm_kgrid_res_body(a_ref, w_ref, r_ref, o_ref):
    part = jnp.dot(a_ref[...], w_ref[...], preferred_element_type=F32)

    @pl.when(pl.program_id(2) == 0)
    def _():
        o_ref[...] = r_ref[...] + part

    @pl.when(pl.program_id(2) != 0)
    def _():
        o_ref[...] += part


def matmul_kgrid_res(a, w, residual, bm, bn, bk):
    m, k = a.shape
    n = w.shape[1]
    return pl.pallas_call(
        _mm_kgrid_res_body,
        grid=(n // bn, m // bm, k // bk),
        in_specs=[
            pl.BlockSpec((bm, bk), lambda j, i, kk: (i, kk)),
            pl.BlockSpec((bk, bn), lambda j, i, kk: (kk, j)),
            pl.BlockSpec((bm, bn), lambda j, i, kk: (i, j)),
        ],
        out_specs=pl.BlockSpec((bm, bn), lambda j, i, kk: (i, j)),
        out_shape=jax.ShapeDtypeStruct((m, n), F32),
        compiler_params=_params("parallel", "parallel", "arbitrary"),
        name="matmul_kgrid_res",
    )(a, w, residual)


def _silu_mul(g, u):
    return g / (1.0 + jnp.exp(-g)) * u


def _mm_swiglu_body(a_ref, wg_ref, wu_ref, o_ref, wgb_ref, wub_ref):
    @pl.when(pl.program_id(1) == 0)
    def _():
        wgb_ref[...] = wg_ref[...].astype(BF16)
        wub_ref[...] = wu_ref[...].astype(BF16)

    a = a_ref[...]
    g = jnp.dot(a, wgb_ref[...], preferred_element_type=F32)
    u = jnp.dot(a, wub_ref[...], preferred_element_type=F32)
    o_ref[...] = _silu_mul(g, u).astype(o_ref.dtype)


def matmul_swiglu(a, w13, bm, bn):
    m, k = a.shape
    f = w13.shape[1] // 2
    nb = f // bn
    return pl.pallas_call(
        _mm_swiglu_body,
        grid=(nb, m // bm),
        in_specs=[
            pl.BlockSpec((bm, k), lambda j, i: (i, 0)),
            pl.BlockSpec((k, bn), lambda j, i: (0, j)),
            pl.BlockSpec((k, bn), lambda j, i: (0, j + nb)),
        ],
        out_specs=pl.BlockSpec((bm, bn), lambda j, i: (i, j)),
        out_shape=jax.ShapeDtypeStruct((m, f), BF16),
        scratch_shapes=[pltpu.VMEM((k, bn), BF16), pltpu.VMEM((k, bn), BF16)],
        compiler_params=_params("parallel", "arbitrary"),
        name="matmul_swiglu",
    )(a, w13, w13)


def _split_bf16(x):
    hi = x.astype(BF16)
    lo = (x - hi.astype(F32)).astype(BF16)
    return hi, lo


def _norm_mm_hp_body(x_ref, g_ref, w_ref, o_ref):
    x = x_ref[...]
    inv = lax.rsqrt(jnp.mean(x * x, axis=-1, keepdims=True) + EPS)
    xn = x * inv * g_ref[...]
    xh, xl = _split_bf16(xn)
    wh, wl = _split_bf16(w_ref[...])
    acc = jnp.dot(xh, wh, preferred_element_type=F32)
    acc += jnp.dot(xh, wl, preferred_element_type=F32)
    acc += jnp.dot(xl, wh, preferred_element_type=F32)
    o_ref[...] = acc


def norm_matmul_hp(x, gain, w, bm=512):
    t, d = x.shape
    n = w.shape[1]
    return pl.pallas_call(
        _norm_mm_hp_body,
        grid=(t // bm,),
        in_specs=[
            pl.BlockSpec((bm, d), lambda i: (i, 0)),
            pl.BlockSpec((1, d), lambda i: (0, 0)),
            pl.BlockSpec((d, n), lambda i: (0, 0)),
        ],
        out_specs=pl.BlockSpec((bm, n), lambda i: (i, 0)),
        out_shape=jax.ShapeDtypeStruct((t, n), F32),
        compiler_params=_params("parallel"),
        name="norm_matmul_hp",
    )(x, gain.reshape(1, d), w)


def _qk_rope_body(p_ref, cos_ref, sin_ref, qg_ref, kg_ref, q_ref, kt_ref, v_ref, *, n_q, n_kv, dh, q_scale):
    cos = cos_ref[...]
    sin = sin_ref[...]
    even = (lax.broadcasted_iota(jnp.int32, cos.shape, 1) % 2) == 0

    def norm_rope(x, gain):
        xn = x * lax.rsqrt(jnp.mean(x * x, axis=-1, keepdims=True) + EPS) * gain
        partner = jnp.where(even, pltpu.roll(xn, dh - 1, 1), pltpu.roll(xn, 1, 1))
        return xn * cos + partner * sin

    for h in range(n_q):
        x = p_ref[:, h * dh:(h + 1) * dh]
        q_ref[:, h * dh:(h + 1) * dh] = (norm_rope(x, qg_ref[...]) * q_scale).astype(q_ref.dtype)
    for h in range(n_kv):
        x = p_ref[:, (n_q + h) * dh:(n_q + h + 1) * dh]
        kt_ref[h * dh:(h + 1) * dh, :] = norm_rope(x, kg_ref[...]).T.astype(kt_ref.dtype)
    v0 = (n_q + n_kv) * dh
    v_ref[...] = p_ref[:, v0:v0 + n_kv * dh].astype(v_ref.dtype)


def qk_norm_rope(proj, cos_full, sin_signed, q_gain, k_gain, batch, seq, n_q, n_kv, dh, bm=256):
    sb = seq // bm
    body = functools.partial(_qk_rope_body, n_q=n_q, n_kv=n_kv, dh=dh, q_scale=dh ** -0.5)
    return pl.pallas_call(
        body,
        grid=(batch, sb),
        in_specs=[
            pl.BlockSpec((bm, proj.shape[1]), lambda b, i: (b * sb + i, 0)),
            pl.BlockSpec((bm, dh), lambda b, i: (i, 0)),
            pl.BlockSpec((bm, dh), lambda b, i: (i, 0)),
            pl.BlockSpec((1, dh), lambda b, i: (0, 0)),
            pl.BlockSpec((1, dh), lambda b, i: (0, 0)),
        ],
        out_specs=[
            pl.BlockSpec((None, bm, n_q * dh), lambda b, i: (b, i, 0)),
            pl.BlockSpec((None, n_kv * dh, bm), lambda b, i: (b, 0, i)),
            pl.BlockSpec((None, bm, n_kv * dh), lambda b, i: (b, i, 0)),
        ],
        out_shape=[
            jax.ShapeDtypeStruct((batch, seq, n_q * dh), BF16),
            jax.ShapeDtypeStruct((batch, n_kv * dh, seq), BF16),
            jax.ShapeDtypeStruct((batch, seq, n_kv * dh), BF16),
        ],
        compiler_params=_params("parallel", "parallel"),
        name="qk_norm_rope",
    )(proj, cos_full, sin_signed, q_gain.reshape(1, dh), k_gain.reshape(1, dh))


def _attn_body(q_ref, kt_ref, v_ref, o_ref, vx_ref, *, group, dh):
    @pl.when(pl.program_id(2) == 0)
    def _():
        vx_ref[:, 0:dh] = v_ref[...]
        vx_ref[:, dh:2 * dh] = jnp.ones((v_ref.shape[0], dh), BF16)

    for g in range(group):
        q = q_ref[:, g * dh:(g + 1) * dh]
        s = jnp.dot(q, kt_ref[...], preferred_element_type=F32)
        m = jnp.max(s, axis=-1, keepdims=True)
        p = jnp.exp(s - m).astype(BF16)
        ox = jnp.dot(p, vx_ref[...], preferred_element_type=F32)
        o = ox[:, 0:dh] * (1.0 / ox[:, dh:dh + 1])
        o_ref[:, g * dh:(g + 1) * dh] = o.astype(o_ref.dtype)


def attention(q, kt, v, n_kv, group, dh, tq=512):
    b, s, _ = q.shape
    body = functools.partial(_attn_body, group=group, dh=dh)
    return pl.pallas_call(
        body,
        grid=(b, n_kv, s // tq),
        in_specs=[
            pl.BlockSpec((None, tq, group * dh), lambda bi, kv, i: (bi, i, kv)),
            pl.BlockSpec((None, dh, s), lambda bi, kv, i: (bi, kv, 0)),
            pl.BlockSpec((None, s, dh), lambda bi, kv, i: (bi, 0, kv)),
        ],
        out_specs=pl.BlockSpec((None, tq, group * dh), lambda bi, kv, i: (bi, i, kv)),
        out_shape=jax.ShapeDtypeStruct(q.shape, BF16),
        scratch_shapes=[pltpu.VMEM((s, 2 * dh), BF16)],
        compiler_params=_params("parallel", "parallel", "arbitrary"),
        name="attention",
    )(q, kt, v)


def _cumsum_dot(x, tri):
    hi = x.astype(BF16)
    r1 = x - hi.astype(F32)
    mid = r1.astype(BF16)
    lo = (r1 - mid.astype(F32)).astype(BF16)
    acc = jnp.dot(hi, tri, preferred_element_type=F32)
    acc += jnp.dot(mid, tri, preferred_element_type=F32)
    acc += jnp.dot(lo, tri, preferred_element_type=F32)
    return acc


def _gates_body(pre_ref, bias_ref, o_ref, *, heads, chunk_shift):
    g = pre_ref[...] + bias_ref[...]
    g = GATE_CAP * jnp.tanh(g / GATE_CAP)
    logsig = jnp.minimum(g, 0.0) - jnp.log1p(jnp.exp(-jnp.abs(g)))
    n = g.shape[1]
    t = lax.broadcasted_iota(jnp.int32, (n, n), 0)
    j = lax.broadcasted_iota(jnp.int32, (n, n), 1)
    same = lax.shift_right_logical(t, chunk_shift) == lax.shift_right_logical(j, chunk_shift)
    tri_f = jnp.where(jnp.logical_and(same, t <= j), 1.0, 0.0).astype(BF16)
    tri_b = jnp.where(jnp.logical_and(same, t >= j), 1.0, 0.0).astype(BF16)
    h = heads
    bc_f = _cumsum_dot(logsig[h:2 * h], tri_f)
    suf_b = _cumsum_dot(logsig[3 * h:4 * h], tri_b)
    o_ref[0:h, :] = bc_f
    o_ref[h:2 * h, :] = g[0:h] - bc_f
    o_ref[2 * h:3 * h, :] = suf_b
    o_ref[3 * h:4 * h, :] = g[2 * h:3 * h] - suf_b


def mlstm_gates(pre_rows, bias, heads, chunk, sb=512):
    b, r, s = pre_rows.shape
    body = functools.partial(_gates_body, heads=heads, chunk_shift=chunk.bit_length() - 1)
    return pl.pallas_call(
        body,
        grid=(b, s // sb),
        in_specs=[
            pl.BlockSpec((None, r, sb), lambda bi, i: (bi, 0, i)),
            pl.BlockSpec((r, 1), lambda bi, i: (0, 0)),
        ],
        out_specs=pl.BlockSpec((None, r, sb), lambda bi, i: (bi, 0, i)),
        out_shape=jax.ShapeDtypeStruct((b, r, s), F32),
        compiler_params=_params("parallel", "parallel"),
        name="mlstm_gates",
    )(pre_rows, bias.reshape(r, 1))


def _mlstm_body(q_ref, k_ref, v_ref, gcol_ref, grow_ref, h_ref, c_ref, n_ref, m_ref, *, heads, q_scale):
    d = pl.program_id(1)
    hd = pl.program_id(2)

    @pl.when(pl.program_id(3) == 0)
    def _():
        c_ref[...] = jnp.zeros(c_ref.shape, F32)
        n_ref[...] = jnp.zeros(n_ref.shape, F32)
        m_ref[...] = jnp.full(m_ref.shape, NEG, F32)

    q = q_ref[...]
    k = k_ref[...]
    v = v_ref[...]
    ln = q.shape[0]
    gcol = gcol_ref[...]
    lane = lax.broadcasted_iota(jnp.int32, gcol.shape, 1)
    base = 2 * heads * d + hd
    bcol = jnp.sum(jnp.where(lane == base, gcol, 0.0), axis=1, keepdims=True)
    rcol = jnp.sum(jnp.where(lane == base + heads, gcol, 0.0), axis=1, keepdims=True)
    rrow = grow_ref[pl.ds(base + heads, 1), :]

    row_i = lax.broadcasted_iota(jnp.int32, (ln, ln), 0)
    col_i = lax.broadcasted_iota(jnp.int32, (ln, ln), 1)
    sign = 1 - 2 * d
    mask = (col_i - row_i) * sign <= 0
    dmat = jnp.where(mask, bcol + rrow, NEG)
    m_prev = m_ref[...]
    inter = bcol + m_prev
    m_row = jnp.maximum(jnp.max(dmat, axis=1, keepdims=True), inter)
    qk = lax.dot_general(q, k, (((1,), (1,)), ((), ())), preferred_element_type=F32) * q_scale
    s = qk * jnp.exp(dmat - m_row)
    decay = jnp.exp(inter - m_row)
    q_c = jnp.dot(q, c_ref[...].astype(BF16), preferred_element_type=F32) * q_scale
    num = jnp.dot(s.astype(BF16), v, preferred_element_type=F32) + decay * q_c
    q_n = jnp.sum(q.astype(F32) * n_ref[...], axis=1, keepdims=True) * q_scale
    den = jnp.sum(s, axis=1, keepdims=True) + decay * q_n
    h_ref[...] = num / jnp.maximum(jnp.abs(den), jnp.exp(-m_row))

    g_last = jnp.where(d == 0, bcol[ln - 1:ln, :], bcol[0:1, :])
    wcol = g_last + rcol
    m_new = jnp.maximum(g_last + m_prev, jnp.max(wcol, axis=0, keepdims=True))
    carry_decay = jnp.exp(g_last + m_prev - m_new)
    wk = jnp.exp(wcol - m_new) * k.astype(F32)
    kv = lax.dot_general(wk.astype(BF16), v, (((0,), (0,)), ((), ())), preferred_element_type=F32)
    c_ref[...] = carry_decay * c_ref[...] + kv
    n_ref[...] = carry_decay * n_ref[...] + jnp.sum(wk, axis=0, keepdims=True)
    m_ref[...] = m_new


def mlstm_scan(proj, gcol, grow, heads, dk, dv, chunk):
    b, s, _ = proj.shape
    nc = s // chunk
    body = functools.partial(_mlstm_body, heads=heads, q_scale=dk ** -0.5)

    def cidx(d, c):
        return c + d * (nc - 1 - 2 * c)

    k_blk0 = heads
    v_blk0 = (2 * heads * dk) // dv
    return pl.pallas_call(
        body,
        grid=(b, 2, heads, nc),
        in_specs=[
            pl.BlockSpec((None, chunk, dk), lambda bi, d, h, c: (bi, cidx(d, c), h)),
            pl.BlockSpec((None, chunk, dk), lambda bi, d, h, c: (bi, cidx(d, c), k_blk0 + h)),
            pl.BlockSpec((None, chunk, dv), lambda bi, d, h, c: (bi, cidx(d, c), v_blk0 + h)),
            pl.BlockSpec((None, chunk, 4 * heads), lambda bi, d, h, c: (bi, cidx(d, c), 0)),
            pl.BlockSpec((None, 4 * heads, chunk), lambda bi, d, h, c: (bi, 0, cidx(d, c))),
        ],
        out_specs=pl.BlockSpec((None, None, chunk, dv), lambda bi, d, h, c: (d, bi, cidx(d, c), h)),
        out_shape=jax.ShapeDtypeStruct((2, b, s, heads * dv), F32),
        scratch_shapes=[
            pltpu.VMEM((dk, dv), F32),
            pltpu.VMEM((1, dk), F32),
            pltpu.VMEM((1, 1), F32),
        ],
        compiler_params=_params("parallel", "parallel", "parallel", "arbitrary"),
        name="mlstm_scan",
    )(proj, proj, proj, gcol, grow)


def _ml_out_body(hf_ref, hb_ref, o_ref, g_ref, out_ref, *, heads, dv):
    for h in range(heads):
        sl = slice(h * dv, (h + 1) * dv)
        x = hf_ref[:, sl] + hb_ref[:, sl]
        xn = x * lax.rsqrt(jnp.mean(x * x, axis=-1, keepdims=True) + EPS) * g_ref[:, sl]
        og = o_ref[:, sl].astype(F32)
        out_ref[:, sl] = (xn / (1.0 + jnp.exp(-og))).astype(out_ref.dtype)


def mlstm_out_gate(h_dirs, proj, h_gain, heads, dv, o_blk, bm=256):
    _, t, d = h_dirs.shape
    body = functools.partial(_ml_out_body, heads=heads, dv=dv)
    return pl.pallas_call(
        body,
        grid=(t // bm,),
        in_specs=[
            pl.BlockSpec((None, bm, d), lambda i: (0, i, 0)),
            pl.BlockSpec((None, bm, d), lambda i: (1, i, 0)),
            pl.BlockSpec((bm, d), lambda i: (i, o_blk)),
            pl.BlockSpec((1, d), lambda i: (0, 0)),
        ],
        out_specs=pl.BlockSpec((bm, d), lambda i: (i, 0)),
        out_shape=jax.ShapeDtypeStruct((t, d), BF16),
        compiler_params=_params("parallel"),
        name="mlstm_out_gate",
    )(h_dirs, h_dirs, proj, h_gain.reshape(1, d))


def _route_body(lg_ref, idx_ref, rank_ref, w_ref, cnt_ref, carry_ref):
    @pl.when(pl.program_id(0) == 0)
    def _():
        carry_ref[...] = jnp.zeros(carry_ref.shape, F32)

    lg = lg_ref[...]
    n_e, tb = lg.shape
    e_iota = lax.broadcasted_iota(jnp.int32, lg.shape, 0).astype(F32)
    t1 = jnp.max(lg, axis=0, keepdims=True)
    i1 = jnp.min(jnp.where(lg == t1, e_iota, float(n_e)), axis=0, keepdims=True)
    first = e_iota == i1
    lg2 = jnp.where(first, -jnp.inf, lg)
    t2 = jnp.max(lg2, axis=0, keepdims=True)
    i2 = jnp.min(jnp.where(lg2 == t2, e_iota, float(n_e)), axis=0, keepdims=True)
    second = e_iota == i2
    e2 = jnp.exp(t2 - t1)
    w_ref[0:1, :] = 1.0 / (1.0 + e2)
    w_ref[1:2, :] = e2 / (1.0 + e2)
    idx_ref[0:1, :] = i1.astype(jnp.int32)
    idx_ref[1:2, :] = i2.astype(jnp.int32)

    assign = jnp.where(first, 1.0, 0.0) + jnp.where(second, 1.0, 0.0)
    tp = lax.broadcasted_iota(jnp.int32, (tb, tb), 0)
    tc = lax.broadcasted_iota(jnp.int32, (tb, tb), 1)
    before = jnp.where(tp < tc, 1.0, 0.0).astype(BF16)
    rank = jnp.dot(assign.astype(BF16), before, preferred_element_type=F32) + carry_ref[:, 0:1]
    rank_ref[0:1, :] = jnp.sum(jnp.where(first, rank, 0.0), axis=0, keepdims=True).astype(jnp.int32)
    rank_ref[1:2, :] = jnp.sum(jnp.where(second, rank, 0.0), axis=0, keepdims=True).astype(jnp.int32)
    carry_ref[...] = carry_ref[...] + jnp.sum(assign, axis=1, keepdims=True)
    cnt_ref[...] = carry_ref[...]


def route_top2(logits_rows, tb=512):
    n_e, t = logits_rows.shape
    return pl.pallas_call(
        _route_body,
        grid=(t // tb,),
        in_specs=[pl.BlockSpec((n_e, tb), lambda i: (0, i))],
        out_specs=[
            pl.BlockSpec((2, tb), lambda i: (0, i)),
            pl.BlockSpec((2, tb), lambda i: (0, i)),
            pl.BlockSpec((2, tb), lambda i: (0, i)),
            pl.BlockSpec((n_e, 128), lambda i: (0, 0)),
        ],
        out_shape=[
            jax.ShapeDtypeStruct((2, t), jnp.int32),
            jax.ShapeDtypeStruct((2, t), jnp.int32),
            jax.ShapeDtypeStruct((2, t), F32),
            jax.ShapeDtypeStruct((n_e, 128), F32),
        ],
        scratch_shapes=[pltpu.VMEM((n_e, 128), F32)],
        compiler_params=_params("arbitrary"),
        name="route_top2",
    )(logits_rows)


def _gather_norm_body(src_ref, used_ref, x_hbm, g_ref, o_ref, xbuf, sem, *, tb):
    i = pl.program_id(0)

    def row_copy(src_row, j):
        return pltpu.make_async_copy(x_hbm.at[pl.ds(src_row, 1)], xbuf.at[pl.ds(j, 1)], sem)

    @pl.when(i < used_ref[0])
    def _():
        def issue(j, carry):
            row_copy(src_ref[i * tb + j], j).start()
            return carry

        lax.fori_loop(0, tb, issue, 0)

        def drain(j, carry):
            row_copy(0, 0).wait()
            return carry

        lax.fori_loop(0, tb, drain, 0)
        x = xbuf[...]
        inv = lax.rsqrt(jnp.mean(x * x, axis=-1, keepdims=True) + EPS)
        o_ref[...] = (x * inv * g_ref[...]).astype(o_ref.dtype)

    @pl.when(i >= used_ref[0])
    def _():
        o_ref[...] = jnp.zeros(o_ref.shape, o_ref.dtype)


def moe_gather_norm(x, gain, src_rows, n_used, tb):
    t, d = x.shape
    r = src_rows.shape[0]
    body = functools.partial(_gather_norm_body, tb=tb)
    return pl.pallas_call(
        body,
        grid_spec=pltpu.PrefetchScalarGridSpec(
            num_scalar_prefetch=2,
            grid=(r // tb,),
            in_specs=[pl.BlockSpec(memory_space=pl.ANY), pl.BlockSpec((1, d), lambda i, src, nu: (0, 0))],
            out_specs=pl.BlockSpec((tb, d), lambda i, src, nu: (i, 0)),
            scratch_shapes=[pltpu.VMEM((tb, d), F32), pltpu.SemaphoreType.DMA(())],
        ),
        out_shape=jax.ShapeDtypeStruct((r, d), BF16),
        compiler_params=_params("arbitrary"),
        name="moe_gather_norm",
    )(src_rows, n_used, x, gain.reshape(1, d))


def _new_weight_block(exp_ref):
    i = pl.program_id(1)
    return jnp.logical_or(i == 0, exp_ref[i] != exp_ref[jnp.maximum(i - 1, 0)])


def _gmm_swiglu_body(src_ref, exp_ref, used_ref, x_ref, wg_ref, wu_ref, o_ref, wgb_ref, wub_ref):
    del src_ref

    @pl.when(_new_weight_block(exp_ref))
    def _():
        wgb_ref[...] = wg_ref[...].astype(BF16)
        wub_ref[...] = wu_ref[...].astype(BF16)

    @pl.when(pl.program_id(1) < used_ref[0])
    def _():
        a = x_ref[...]
        g = jnp.dot(a, wgb_ref[...], preferred_element_type=F32)
        u = jnp.dot(a, wub_ref[...], preferred_element_type=F32)
        o_ref[...] = _silu_mul(g, u).astype(o_ref.dtype)

    @pl.when(pl.program_id(1) >= used_ref[0])
    def _():
        o_ref[...] = jnp.zeros(o_ref.shape, o_ref.dtype)


def grouped_swiglu(xs, w13, tile_src, tile_exp, n_used, bm, bn):
    r, d = xs.shape
    f = w13.shape[2] // 2
    nb = f // bn
    return pl.pallas_call(
        _gmm_swiglu_body,
        grid_spec=pltpu.PrefetchScalarGridSpec(
            num_scalar_prefetch=3,
            grid=(nb, r // bm),
            in_specs=[
                pl.BlockSpec((bm, d), lambda j, i, src, ex, nu: (src[i], 0)),
                pl.BlockSpec((None, d, bn), lambda j, i, src, ex, nu: (ex[i], 0, j)),
                pl.BlockSpec((None, d, bn), lambda j, i, src, ex, nu: (ex[i], 0, j + nb)),
            ],
            out_specs=pl.BlockSpec((bm, bn), lambda j, i, src, ex, nu: (i, j)),
            scratch_shapes=[pltpu.VMEM((d, bn), BF16), pltpu.VMEM((d, bn), BF16)],
        ),
        out_shape=jax.ShapeDtypeStruct((r, f), BF16),
        compiler_params=_params("arbitrary", "arbitrary"),
        name="grouped_swiglu",
    )(tile_src, tile_exp, n_used, xs, w13, w13)


def _gmm_body(src_ref, exp_ref, used_ref, a_ref, w_ref, o_ref, wb_ref):
    del src_ref

    @pl.when(_new_weight_block(exp_ref))
    def _():
        wb_ref[...] = w_ref[...].astype(BF16)

    @pl.when(pl.program_id(1) < used_ref[0])
    def _():
        o_ref[...] = jnp.dot(a_ref[...], wb_ref[...], preferred_element_type=F32).astype(o_ref.dtype)

    @pl.when(pl.program_id(1) >= used_ref[0])
    def _():
        o_ref[...] = jnp.zeros(o_ref.shape, o_ref.dtype)


def grouped_matmul(a, w, tile_src, tile_exp, n_used, bm, bn):
    r, k = a.shape
    n = w.shape[2]
    return pl.pallas_call(
        _gmm_body,
        grid_spec=pltpu.PrefetchScalarGridSpec(
            num_scalar_prefetch=3,
            grid=(n // bn, r // bm),
            in_specs=[
                pl.BlockSpec((bm, k), lambda j, i, src, ex, nu: (src[i], 0)),
                pl.BlockSpec((None, k, bn), lambda j, i, src, ex, nu: (ex[i], 0, j)),
            ],
            out_specs=pl.BlockSpec((bm, bn), lambda j, i, src, ex, nu: (i, j)),
            scratch_shapes=[pltpu.VMEM((k, bn), BF16)],
        ),
        out_shape=jax.ShapeDtypeStruct((r, n), F32),
        compiler_params=_params("arbitrary", "arbitrary"),
        name="grouped_matmul",
    )(tile_src, tile_exp, n_used, a, w)


def _combine_body(pos_ref, x_ref, w_ref, y_hbm, o_ref, ybuf, sem, *, tb, t_total):
    t0 = pl.program_id(0) * tb

    def row_copy(src_row, choice, j):
        return pltpu.make_async_copy(y_hbm.at[pl.ds(src_row, 1)], ybuf.at[choice, pl.ds(j, 1)], sem)

    def issue(j, carry):
        t = t0 + j
        row_copy(pos_ref[t], 0, j).start()
        row_copy(pos_ref[t_total + t], 1, j).start()
        return carry

    lax.fori_loop(0, tb, issue, 0)

    def drain(j, carry):
        row_copy(0, 0, 0).wait()
        row_copy(0, 1, 0).wait()
        return carry

    lax.fori_loop(0, tb, drain, 0)
    w = w_ref[...]
    o_ref[...] = x_ref[...] + w[:, 0:1] * ybuf[0] + w[:, 1:2] * ybuf[1]


def moe_combine(x, y, pos_flat, w_cols, tb=256):
    t, d = x.shape
    body = functools.partial(_combine_body, tb=tb, t_total=t)
    return pl.pallas_call(
        body,
        grid_spec=pltpu.PrefetchScalarGridSpec(
            num_scalar_prefetch=1,
            grid=(t // tb,),
            in_specs=[
                pl.BlockSpec((tb, d), lambda i, pos: (i, 0)),
                pl.BlockSpec((tb, 2), lambda i, pos: (i, 0)),
                pl.BlockSpec(memory_space=pl.ANY),
            ],
            out_specs=pl.BlockSpec((tb, d), lambda i, pos: (i, 0)),
            scratch_shapes=[pltpu.VMEM((2, tb, d), F32), pltpu.SemaphoreType.DMA(())],
        ),
        out_shape=jax.ShapeDtypeStruct((t, d), F32),
        compiler_params=_params("arbitrary"),
        name="moe_combine",
    )(pos_flat, x, w_cols, y)


def moe_plan(idx, rank, counts, bm, max_tiles):
    t = idx.shape[1]
    tiles = (counts + bm - 1) // bm
    tile_end = jnp.cumsum(tiles)
    row_off = (tile_end - tiles) * bm
    pos = (row_off[idx] + rank).reshape(-1).astype(jnp.int32)
    tokens = jnp.tile(jnp.arange(t, dtype=jnp.int32), 2)
    src_rows = jnp.zeros((max_tiles * bm,), jnp.int32).at[pos].set(tokens)
    n_used = tile_end[-1]
    tile_src = jnp.minimum(jnp.arange(max_tiles, dtype=jnp.int32), n_used - 1)
    tile_exp = jnp.sum(tile_src[:, None] >= tile_end[None, :], axis=1).astype(jnp.int32)
    return pos, src_rows, tile_src, tile_exp, n_used.reshape(1).astype(jnp.int32)


def _rope_tables(seq):
    rows = seq // GRID_W
    row_ids = jnp.repeat(jnp.arange(rows, dtype=F32), GRID_W)
    col_ids = jnp.tile(jnp.arange(GRID_W, dtype=F32), rows)
    n_freq = ATT_HEAD_DIM // 4
    inv_freq = ROPE_THETA ** (-jnp.arange(n_freq, dtype=F32) / n_freq)
    ang = jnp.concatenate([row_ids[:, None] * inv_freq, col_ids[:, None] * inv_freq], axis=-1)
    cos, sin = jnp.cos(ang), jnp.sin(ang)
    cos_full = jnp.repeat(cos, 2, axis=-1)
    sin_signed = jnp.stack([-sin, sin], axis=-1).reshape(seq, ATT_HEAD_DIM)
    return cos_full, sin_signed


def _attention_layer(x, b, s, norm_g, w_in, q_gain, k_gain, w_out):
    t, d = x.shape
    hn = rmsnorm(x, norm_g, BF16)
    proj = matmul(hn, w_in, w_in.shape[1], F32, bm=1024, bn=512)
    cos_full, sin_signed = _rope_tables(s)
    q, kt, v = qk_norm_rope(proj, cos_full, sin_signed, q_gain, k_gain, b, s,
                            ATT_HEADS, ATT_KV_HEADS, ATT_HEAD_DIM)
    o = attention(q, kt, v, ATT_KV_HEADS, ATT_GROUP, ATT_HEAD_DIM)
    return matmul(o.reshape(t, -1), w_out, d, F32, bm=1024, bn=512, residual=x)


def _dense_ffn_layer(x, norm_g, w13, w2):
    hn = rmsnorm(x, norm_g, BF16)
    act = matmul_swiglu(hn, w13, bm=512, bn=512)
    return matmul_kgrid_res(act, w2.astype(BF16), x, bm=1024, bn=1024, bk=2048)


def _mlstm_layer(x, b, s, norm_g, w_in, gate_bias, h_gain, w_out):
    t, d = x.shape
    n_main = 2 * ML_HEADS * ML_QK_DIM + 2 * ML_HEADS * ML_V_DIM
    hn = rmsnorm(x, norm_g, BF16)
    proj = matmul(hn, w_in, n_main, BF16, bm=1024, bn=512)
    pre = norm_matmul_hp(x, norm_g, w_in[:, n_main:])
    pre_rows = pre.reshape(b, s, 4 * ML_HEADS).transpose(0, 2, 1)
    grow = mlstm_gates(pre_rows, gate_bias, ML_HEADS, ML_CHUNK)
    gcol = grow.transpose(0, 2, 1)
    h_dirs = mlstm_scan(proj.reshape(b, s, n_main), gcol, grow, ML_HEADS, ML_QK_DIM, ML_V_DIM, ML_CHUNK)
    o_blk = (2 * ML_HEADS * ML_QK_DIM + ML_HEADS * ML_V_DIM) // (ML_HEADS * ML_V_DIM)
    gated = mlstm_out_gate(h_dirs.reshape(2, t, -1), proj, h_gain, ML_HEADS, ML_V_DIM, o_blk)
    return matmul(gated, w_out, d, F32, bm=1024, bn=512, residual=x)


def _moe_layer(x, norm_g, router, w13, w2):
    t, d = x.shape
    logits = norm_matmul_hp(x, norm_g, router)
    idx, rank, w_rows, cnt = route_top2(logits.T)
    max_tiles = (2 * t) // MOE_BM + N_EXPERTS
    pos, src_rows, tile_src, tile_exp, n_used = moe_plan(idx, rank, cnt[:, 0].astype(jnp.int32), MOE_BM, max_tiles)
    xs = moe_gather_norm(x, norm_g, src_rows, n_used, MOE_BM)
    act = grouped_swiglu(xs, w13, tile_src, tile_exp, n_used, MOE_BM, 512)
    y = grouped_matmul(act, w2, tile_src, tile_exp, n_used, MOE_BM, 512)
    return moe_combine(x, y, pos, w_rows.T)


def kernel(x, norm_mix, norm_ffn, att_w_in, att_q_gain, att_k_gain, att_w_out, ffn_w13, ffn_w2,
           ml_w_in, ml_gate_bias, ml_h_gain, ml_w_out, moe_router, moe_w13, moe_w2):
    b, s, d = x.shape
    h = x.reshape(b * s, d)
    h = _attention_layer(h, b, s, norm_mix[0], att_w_in[0], att_q_gain[0], att_k_gain[0], att_w_out[0])
    h = _dense_ffn_layer(h, norm_ffn[0], ffn_w13[0], ffn_w2[0])
    h = _mlstm_layer(h, b, s, norm_mix[1], ml_w_in[0], ml_gate_bias[0], ml_h_gain[0], ml_w_out[0])
    h = _moe_layer(h, norm_ffn[1], moe_router[0], moe_w13[0], moe_w2[0])
    return h.reshape(b, s, d)
```

```python
import functools

import jax
import jax.numpy as jnp
from jax import lax
from jax.experimental import pallas as pl
from jax.experimental.pallas import tpu as pltpu

F32 = jnp.float32
BF16 = jnp.bfloat16

EPS = 1e-6
NEG = -1e30

GRID_W = 64
ROPE_THETA = 10000.0
ATT_HEADS = 32
ATT_KV_HEADS = 8
ATT_HEAD_DIM = 128
ATT_GROUP = ATT_HEADS // ATT_KV_HEADS

ML_HEADS = 8
ML_QK_DIM = 256
ML_V_DIM = 512
GATE_CAP = 15.0
ML_CHUNK = 256

N_EXPERTS = 8
MOE_BM = 256

VMEM_LIMIT_BYTES = 56 * 1024 * 1024


def _params(*sem):
    return pltpu.CompilerParams(dimension_semantics=sem, vmem_limit_bytes=VMEM_LIMIT_BYTES)


def _rmsnorm_body(x_ref, g_ref, o_ref):
    x = x_ref[...]
    inv = lax.rsqrt(jnp.mean(x * x, axis=-1, keepdims=True) + EPS)
    o_ref[...] = (x * inv * g_ref[...]).astype(o_ref.dtype)


def rmsnorm(x, gain, out_dtype, bm=256):
    t, d = x.shape
    return pl.pallas_call(
        _rmsnorm_body,
        grid=(t // bm,),
        in_specs=[pl.BlockSpec((bm, d), lambda i: (i, 0)), pl.BlockSpec((1, d), lambda i: (0, 0))],
        out_specs=pl.BlockSpec((bm, d), lambda i: (i, 0)),
        out_shape=jax.ShapeDtypeStruct((t, d), out_dtype),
        compiler_params=_params("parallel"),
        name="rmsnorm",
    )(x, gain.reshape(1, d))


def _mm_body(a_ref, w_ref, o_ref, wb_ref):
    @pl.when(pl.program_id(1) == 0)
    def _():
        wb_ref[...] = w_ref[...].astype(BF16)

    o_ref[...] = jnp.dot(a_ref[...], wb_ref[...], preferred_element_type=F32).astype(o_ref.dtype)


def _mm_res_body(a_ref, w_ref, r_ref, o_ref, wb_ref):
    @pl.when(pl.program_id(1) == 0)
    def _():
        wb_ref[...] = w_ref[...].astype(BF16)

    o_ref[...] = r_ref[...] + jnp.dot(a_ref[...], wb_ref[...], preferred_element_type=F32)


def matmul(a, w, n, out_dtype, bm, bn, residual=None):
    m, k = a.shape
    in_specs = [pl.BlockSpec((bm, k), lambda j, i: (i, 0)), pl.BlockSpec((k, bn), lambda j, i: (0, j))]
    args = [a, w]
    body = _mm_body
    if residual is not None:
        in_specs.append(pl.BlockSpec((bm, bn), lambda j, i: (i, j)))
        args.append(residual)
        body = _mm_res_body
    return pl.pallas_call(
        body,
        grid=(n // bn, m // bm),
        in_specs=in_specs,
        out_specs=pl.BlockSpec((bm, bn), lambda j, i: (i, j)),
        out_shape=jax.ShapeDtypeStruct((m, n), out_dtype),
        scratch_shapes=[pltpu.VMEM((k, bn), BF16)],
        compiler_params=_params("parallel", "arbitrary"),
        name="matmul",
    )(*args)


def _mm_kgrid_res_body(a_ref, w_ref, r_ref, o_ref):
    part = jnp.dot(a_ref[...], w_ref[...].astype(BF16), preferred_element_type=F32)

    @pl.when(pl.program_id(2) == 0)
    def _():
        o_ref[...] = r_ref[...] + part

    @pl.when(pl.program_id(2) != 0)
    def _():
        o_ref[...] += part


def matmul_kgrid_res(a, w, residual, bm, bn, bk):
    m, k = a.shape
    n = w.shape[1]
    return pl.pallas_call(
        _mm_kgrid_res_body,
        grid=(n // bn, m // bm, k // bk),
        in_specs=[
            pl.BlockSpec((bm, bk), lambda j, i, kk: (i, kk)),
            pl.BlockSpec((bk, bn), lambda j, i, kk: (kk, j)),
            pl.BlockSpec((bm, bn), lambda j, i, kk: (i, j)),
        ],
        out_specs=pl.BlockSpec((bm, bn), lambda j, i, kk: (i, j)),
        out_shape=jax.ShapeDtypeStruct((m, n), F32),
        compiler_params=_params("parallel", "parallel", "arbitrary"),
        name="matmul_kgrid_res",
    )(a, w, residual)


def _silu_mul(g, u):
    return g / (1.0 + jnp.exp(-g)) * u


def _mm_swiglu_body(a_ref, wg_ref, wu_ref, o_ref, wgb_ref, wub_ref):
    @pl.when(pl.program_id(1) == 0)
    def _():
        wgb_ref[...] = wg_ref[...].astype(BF16)
        wub_ref[...] = wu_ref[...].astype(BF16)

    a = a_ref[...]
    g = jnp.dot(a, wgb_ref[...], preferred_element_type=F32)
    u = jnp.dot(a, wub_ref[...], preferred_element_type=F32)
    o_ref[...] = _silu_mul(g, u).astype(o_ref.dtype)


def matmul_swiglu(a, w13, bm, bn):
    m, k = a.shape
    f = w13.shape[1] // 2
    nb = f // bn
    return pl.pallas_call(
        _mm_swiglu_body,
        grid=(nb, m // bm),
        in_specs=[
            pl.BlockSpec((bm, k), lambda j, i: (i, 0)),
            pl.BlockSpec((k, bn), lambda j, i: (0, j)),
            pl.BlockSpec((k, bn), lambda j, i: (0, j + nb)),
        ],
        out_specs=pl.BlockSpec((bm, bn), lambda j, i: (i, j)),
        out_shape=jax.ShapeDtypeStruct((m, f), BF16),
        scratch_shapes=[pltpu.VMEM((k, bn), BF16), pltpu.VMEM((k, bn), BF16)],
        compiler_params=_params("parallel", "arbitrary"),
        name="matmul_swiglu",
    )(a, w13, w13)


def _split_bf16(x):
    hi = x.astype(BF16)
    lo = (x - hi.astype(F32)).astype(BF16)
    return hi, lo


def _norm_mm_hp_body(x_ref, g_ref, w_ref, o_ref):
    x = x_ref[...]
    inv = lax.rsqrt(jnp.mean(x * x, axis=-1, keepdims=True) + EPS)
    xn = x * inv * g_ref[...]
    xh, xl = _split_bf16(xn)
    wh, wl = _split_bf16(w_ref[...])
    acc = jnp.dot(xh, wh, preferred_element_type=F32)
    acc += jnp.dot(xh, wl, preferred_element_type=F32)
    acc += jnp.dot(xl, wh, preferred_element_type=F32)
    o_ref[...] = acc


def norm_matmul_hp(x, gain, w, bm=512):
    t, d = x.shape
    n = w.shape[1]
    return pl.pallas_call(
        _norm_mm_hp_body,
        grid=(t // bm,),
        in_specs=[
            pl.BlockSpec((bm, d), lambda i: (i, 0)),
            pl.BlockSpec((1, d), lambda i: (0, 0)),
            pl.BlockSpec((d, n), lambda i: (0, 0)),
        ],
        out_specs=pl.BlockSpec((bm, n), lambda i: (i, 0)),
        out_shape=jax.ShapeDtypeStruct((t, n), F32),
        compiler_params=_params("parallel"),
        name="norm_matmul_hp",
    )(x, gain.reshape(1, d), w)


def _qk_rope_body(p_ref, cos_ref, sin_ref, qg_ref, kg_ref, q_ref, kt_ref, v_ref, *, n_q, n_kv, dh, q_scale):
    cos = cos_ref[...]
    sin = sin_ref[...]
    pr = lax.broadcasted_iota(jnp.int32, (dh, dh), 0)
    pc = lax.broadcasted_iota(jnp.int32, (dh, dh), 1)
    swap = jnp.where(jnp.bitwise_xor(pr, 1) == pc, 1.0, 0.0).astype(BF16)

    def norm_rope(x, gain):
        xn = x * lax.rsqrt(jnp.mean(x * x, axis=-1, keepdims=True) + EPS) * gain
        partner = jnp.dot(xn.astype(BF16), swap, preferred_element_type=F32)
        return xn * cos + partner * sin

    for h in range(n_q):
        x = p_ref[:, h * dh:(h + 1) * dh]
        q_ref[:, h * dh:(h + 1) * dh] = (norm_rope(x, qg_ref[...]) * q_scale).astype(q_ref.dtype)
    for h in range(n_kv):
        x = p_ref[:, (n_q + h) * dh:(n_q + h + 1) * dh]
        kt_ref[h * dh:(h + 1) * dh, :] = norm_rope(x, kg_ref[...]).T.astype(kt_ref.dtype)
    v0 = (n_q + n_kv) * dh
    v_ref[...] = p_ref[:, v0:v0 + n_kv * dh].astype(v_ref.dtype)


def qk_norm_rope(proj, cos_full, sin_signed, q_gain, k_gain, batch, seq, n_q, n_kv, dh, bm=256):
    sb = seq // bm
    body = functools.partial(_qk_rope_body, n_q=n_q, n_kv=n_kv, dh=dh, q_scale=dh ** -0.5)
    return pl.pallas_call(
        body,
        grid=(batch, sb),
        in_specs=[
            pl.BlockSpec((bm, proj.shape[1]), lambda b, i: (b * sb + i, 0)),
            pl.BlockSpec((bm, dh), lambda b, i: (i, 0)),
            pl.BlockSpec((bm, dh), lambda b, i: (i, 0)),
            pl.BlockSpec((1, dh), lambda b, i: (0, 0)),
            pl.BlockSpec((1, dh), lambda b, i: (0, 0)),
        ],
        out_specs=[
            pl.BlockSpec((None, bm, n_q * dh), lambda b, i: (b, i, 0)),
            pl.BlockSpec((None, n_kv * dh, bm), lambda b, i: (b, 0, i)),
            pl.BlockSpec((None, bm, n_kv * dh), lambda b, i: (b, i, 0)),
        ],
        out_shape=[
            jax.ShapeDtypeStruct((batch, seq, n_q * dh), BF16),
            jax.ShapeDtypeStruct((batch, n_kv * dh, seq), BF16),
            jax.ShapeDtypeStruct((batch, seq, n_kv * dh), BF16),
        ],
        compiler_params=_params("parallel", "parallel"),
        name="qk_norm_rope",
    )(proj, cos_full, sin_signed, q_gain.reshape(1, dh), k_gain.reshape(1, dh))


def _attn_body(q_ref, kt_ref, v_ref, o_ref, vx_ref, *, group, dh):
    @pl.when(pl.program_id(2) == 0)
    def _():
        vx_ref[:, 0:dh] = v_ref[...]
        vx_ref[:, dh:2 * dh] = jnp.ones((v_ref.shape[0], dh), BF16)

    for g in range(group):
        q = q_ref[:, g * dh:(g + 1) * dh]
        s = jnp.dot(q, kt_ref[...], preferred_element_type=F32)
        m = jnp.max(s, axis=-1, keepdims=True)
        p = jnp.exp(s - m).astype(BF16)
        ox = jnp.dot(p, vx_ref[...], preferred_element_type=F32)
        o = ox[:, 0:dh] * (1.0 / ox[:, dh:dh + 1])
        o_ref[:, g * dh:(g + 1) * dh] = o.astype(o_ref.dtype)


def attention(q, kt, v, n_kv, group, dh, tq=512):
    b, s, _ = q.shape
    body = functools.partial(_attn_body, group=group, dh=dh)
    return pl.pallas_call(
        body,
        grid=(b, n_kv, s // tq),
        in_specs=[
            pl.BlockSpec((None, tq, group * dh), lambda bi, kv, i: (bi, i, kv)),
            pl.BlockSpec((None, dh, s), lambda bi, kv, i: (bi, kv, 0)),
            pl.BlockSpec((None, s, dh), lambda bi, kv, i: (bi, 0, kv)),
        ],
        out_specs=pl.BlockSpec((None, tq, group * dh), lambda bi, kv, i: (bi, i, kv)),
        out_shape=jax.ShapeDtypeStruct(q.shape, BF16),
        scratch_shapes=[pltpu.VMEM((s, 2 * dh), BF16)],
        compiler_params=_params("parallel", "parallel", "arbitrary"),
        name="attention",
    )(q, kt, v)


def _cumsum_dot(x, tri):
    hi = x.astype(BF16)
    r1 = x - hi.astype(F32)
    mid = r1.astype(BF16)
    lo = (r1 - mid.astype(F32)).astype(BF16)
    acc = jnp.dot(hi, tri, preferred_element_type=F32)
    acc += jnp.dot(mid, tri, preferred_element_type=F32)
    acc += jnp.dot(lo, tri, preferred_element_type=F32)
    return acc


def _gates_body(pre_ref, bias_ref, o_ref, *, heads, chunk_shift):
    g = pre_ref[...] + bias_ref[...]
    g = GATE_CAP * jnp.tanh(g / GATE_CAP)
    logsig = jnp.minimum(g, 0.0) - jnp.log1p(jnp.exp(-jnp.abs(g)))
    n = g.shape[1]
    t = lax.broadcasted_iota(jnp.int32, (n, n), 0)
    j = lax.broadcasted_iota(jnp.int32, (n, n), 1)
    same = lax.shift_right_logical(t, chunk_shift) == lax.shift_right_logical(j, chunk_shift)
    tri_f = jnp.where(jnp.logical_and(same, t <= j), 1.0, 0.0).astype(BF16)
    tri_b = jnp.where(jnp.logical_and(same, t >= j), 1.0, 0.0).astype(BF16)
    h = heads
    bc_f = _cumsum_dot(logsig[h:2 * h], tri_f)
    suf_b = _cumsum_dot(logsig[3 * h:4 * h], tri_b)
    o_ref[0:h, :] = bc_f
    o_ref[h:2 * h, :] = g[0:h] - bc_f
    o_ref[2 * h:3 * h, :] = suf_b
    o_ref[3 * h:4 * h, :] = g[2 * h:3 * h] - suf_b


def mlstm_gates(pre_rows, bias, heads, chunk, sb=512):
    b, r, s = pre_rows.shape
    body = functools.partial(_gates_body, heads=heads, chunk_shift=chunk.bit_length() - 1)
    return pl.pallas_call(
        body,
        grid=(b, s // sb),
        in_specs=[
            pl.BlockSpec((None, r, sb), lambda bi, i: (bi, 0, i)),
            pl.BlockSpec((r, 1), lambda bi, i: (0, 0)),
        ],
        out_specs=pl.BlockSpec((None, r, sb), lambda bi, i: (bi, 0, i)),
        out_shape=jax.ShapeDtypeStruct((b, r, s), F32),
        compiler_params=_params("parallel", "parallel"),
        name="mlstm_gates",
    )(pre_rows, bias.reshape(r, 1))


def _mlstm_chunk(q_ref, k_ref, v_ref, gcol_ref, grow_ref, h_ref, c_ref, n_ref, m_ref, hd, *, heads, q_scale, backward):
    q = q_ref[...]
    k = k_ref[...]
    v = v_ref[...]
    ln = q.shape[0]
    gcol = gcol_ref[...]
    lane = lax.broadcasted_iota(jnp.int32, gcol.shape, 1)
    base = (2 * heads if backward else 0) + hd
    bcol = jnp.sum(jnp.where(lane == base, gcol, 0.0), axis=1, keepdims=True)
    rcol = jnp.sum(jnp.where(lane == base + heads, gcol, 0.0), axis=1, keepdims=True)
    rrow = grow_ref[pl.ds(base + heads, 1), :]

    row_i = lax.broadcasted_iota(jnp.int32, (ln, ln), 0)
    col_i = lax.broadcasted_iota(jnp.int32, (ln, ln), 1)
    mask = (col_i >= row_i) if backward else (col_i <= row_i)
    dmat = jnp.where(mask, bcol + rrow, NEG)
    m_prev = m_ref[...]
    inter = bcol + m_prev
    m_row = jnp.maximum(jnp.max(dmat, axis=1, keepdims=True), inter)
    qk = lax.dot_general(q, k, (((1,), (1,)), ((), ())), preferred_element_type=F32) * q_scale
    s = qk * jnp.exp(dmat - m_row)
    decay = jnp.exp(inter - m_row)
    q_c = jnp.dot(q, c_ref[...].astype(BF16), preferred_element_type=F32) * q_scale
    num = jnp.dot(s.astype(BF16), v, preferred_element_type=F32) + decay * q_c
    q_n = jnp.sum(q.astype(F32) * n_ref[...], axis=1, keepdims=True) * q_scale
    den = jnp.sum(s, axis=1, keepdims=True) + decay * q_n
    h_ref[...] = num / jnp.maximum(jnp.abs(den), jnp.exp(-m_row))

    g_last = bcol[0:1, :] if backward else bcol[ln - 1:ln, :]
    wcol = g_last + rcol
    m_new = jnp.maximum(g_last + m_prev, jnp.max(wcol, axis=0, keepdims=True))
    carry_decay = jnp.exp(g_last + m_prev - m_new)
    wk = jnp.exp(wcol - m_new) * k.astype(F32)
    kv = lax.dot_general(wk.astype(BF16), v, (((0,), (0,)), ((), ())), preferred_element_type=F32)
    c_ref[...] = carry_decay * c_ref[...] + kv
    n_ref[...] = carry_decay * n_ref[...] + jnp.sum(wk, axis=0, keepdims=True)
    m_ref[...] = m_new


def _mlstm_body(qf_ref, kf_ref, vf_ref, gcf_ref, grf_ref, qb_ref, kb_ref, vb_ref, gcb_ref, grb_ref,
                hf_ref, hb_ref, c_ref, n_ref, m_ref, *, heads, q_scale):
    hd = pl.program_id(1)

    @pl.when(pl.program_id(2) == 0)
    def _():
        c_ref[...] = jnp.zeros(c_ref.shape, F32)
        n_ref[...] = jnp.zeros(n_ref.shape, F32)
        m_ref[...] = jnp.full(m_ref.shape, NEG, F32)

    chunk = functools.partial(_mlstm_chunk, heads=heads, q_scale=q_scale)
    chunk(qf_ref, kf_ref, vf_ref, gcf_ref, grf_ref, hf_ref, c_ref.at[0], n_ref.at[0], m_ref.at[0], hd,
          backward=False)
    chunk(qb_ref, kb_ref, vb_ref, gcb_ref, grb_ref, hb_ref, c_ref.at[1], n_ref.at[1], m_ref.at[1], hd,
          backward=True)


def mlstm_scan(proj, gcol, grow, heads, dk, dv, chunk):
    b, s, _ = proj.shape
    nc = s // chunk
    body = functools.partial(_mlstm_body, heads=heads, q_scale=dk ** -0.5)
    k_blk0 = heads
    v_blk0 = (2 * heads * dk) // dv

    def direction_specs(cidx):
        return [
            pl.BlockSpec((None, chunk, dk), lambda bi, h, c: (bi, cidx(c), h)),
            pl.BlockSpec((None, chunk, dk), lambda bi, h, c: (bi, cidx(c), k_blk0 + h)),
            pl.BlockSpec((None, chunk, dv), lambda bi, h, c: (bi, cidx(c), v_blk0 + h)),
            pl.BlockSpec((None, chunk, 4 * heads), lambda bi, h, c: (bi, cidx(c), 0)),
            pl.BlockSpec((None, 4 * heads, chunk), lambda bi, h, c: (bi, 0, cidx(c))),
        ]

    def fwd(c):
        return c

    def bwd(c):
        return nc - 1 - c

    out_sds = jax.ShapeDtypeStruct((b, s, heads * dv), F32)
    return pl.pallas_call(
        body,
        grid=(b, heads, nc),
        in_specs=direction_specs(fwd) + direction_specs(bwd),
        out_specs=[
            pl.BlockSpec((None, chunk, dv), lambda bi, h, c: (bi, fwd(c), h)),
            pl.BlockSpec((None, chunk, dv), lambda bi, h, c: (bi, bwd(c), h)),
        ],
        out_shape=[out_sds, out_sds],
        scratch_shapes=[
            pltpu.VMEM((2, dk, dv), F32),
            pltpu.VMEM((2, 1, dk), F32),
            pltpu.VMEM((2, 1, 1), F32),
        ],
        compiler_params=_params("parallel", "parallel", "arbitrary"),
        name="mlstm_scan",
    )(*([proj, proj, proj, gcol, grow] * 2))


def _ml_out_body(hf_ref, hb_ref, o_ref, g_ref, out_ref, *, heads, dv):
    for h in range(heads):
        sl = slice(h * dv, (h + 1) * dv)
        x = hf_ref[:, sl] + hb_ref[:, sl]
        xn = x * lax.rsqrt(jnp.mean(x * x, axis=-1, keepdims=True) + EPS) * g_ref[:, sl]
        og = o_ref[:, sl].astype(F32)
        out_ref[:, sl] = (xn / (1.0 + jnp.exp(-og))).astype(out_ref.dtype)


def mlstm_out_gate(h_fwd, h_bwd, proj, h_gain, heads, dv, o_blk, bm=256):
    t, d = h_fwd.shape
    body = functools.partial(_ml_out_body, heads=heads, dv=dv)
    return pl.pallas_call(
        body,
        grid=(t // bm,),
        in_specs=[
            pl.BlockSpec((bm, d), lambda i: (i, 0)),
            pl.BlockSpec((bm, d), lambda i: (i, 0)),
            pl.BlockSpec((bm, d), lambda i: (i, o_blk)),
            pl.BlockSpec((1, d), lambda i: (0, 0)),
        ],
        out_specs=pl.BlockSpec((bm, d), lambda i: (i, 0)),
        out_shape=jax.ShapeDtypeStruct((t, d), BF16),
        compiler_params=_params("parallel"),
        name="mlstm_out_gate",
    )(h_fwd, h_bwd, proj, h_gain.reshape(1, d))


def _route_body(lg_ref, idx_ref, rank_ref, w_ref, cnt_ref, carry_ref):
    @pl.when(pl.program_id(0) == 0)
    def _():
        carry_ref[...] = jnp.zeros(carry_ref.shape, F32)

    lg = lg_ref[...]
    n_e, tb = lg.shape
    e_iota = lax.broadcasted_iota(jnp.int32, lg.shape, 0).astype(F32)
    t1 = jnp.max(lg, axis=0, keepdims=True)
    i1 = jnp.min(jnp.where(lg == t1, e_iota, float(n_e)), axis=0, keepdims=True)
    first = e_iota == i1
    lg2 = jnp.where(first, -jnp.inf, lg)
    t2 = jnp.max(lg2, axis=0, keepdims=True)
    i2 = jnp.min(jnp.where(lg2 == t2, e_iota, float(n_e)), axis=0, keepdims=True)
    second = e_iota == i2
    e2 = jnp.exp(t2 - t1)
    w_ref[0:1, :] = 1.0 / (1.0 + e2)
    w_ref[1:2, :] = e2 / (1.0 + e2)
    idx_ref[0:1, :] = i1.astype(jnp.int32)
    idx_ref[1:2, :] = i2.astype(jnp.int32)

    assign = jnp.where(first, 1.0, 0.0) + jnp.where(second, 1.0, 0.0)
    tp = lax.broadcasted_iota(jnp.int32, (tb, tb), 0)
    tc = lax.broadcasted_iota(jnp.int32, (tb, tb), 1)
    before = jnp.where(tp < tc, 1.0, 0.0).astype(BF16)
    rank = jnp.dot(assign.astype(BF16), before, preferred_element_type=F32) + carry_ref[:, 0:1]
    rank_ref[0:1, :] = jnp.sum(jnp.where(first, rank, 0.0), axis=0, keepdims=True).astype(jnp.int32)
    rank_ref[1:2, :] = jnp.sum(jnp.where(second, rank, 0.0), axis=0, keepdims=True).astype(jnp.int32)
    carry_ref[...] = carry_ref[...] + jnp.sum(assign, axis=1, keepdims=True)
    cnt_ref[...] = carry_ref[...]


def route_top2(logits_rows, tb=512):
    n_e, t = logits_rows.shape
    return pl.pallas_call(
        _route_body,
        grid=(t // tb,),
        in_specs=[pl.BlockSpec((n_e, tb), lambda i: (0, i))],
        out_specs=[
            pl.BlockSpec((2, tb), lambda i: (0, i)),
            pl.BlockSpec((2, tb), lambda i: (0, i)),
            pl.BlockSpec((2, tb), lambda i: (0, i)),
            pl.BlockSpec((n_e, 128), lambda i: (0, 0)),
        ],
        out_shape=[
            jax.ShapeDtypeStruct((2, t), jnp.int32),
            jax.ShapeDtypeStruct((2, t), jnp.int32),
            jax.ShapeDtypeStruct((2, t), F32),
            jax.ShapeDtypeStruct((n_e, 128), F32),
        ],
        scratch_shapes=[pltpu.VMEM((n_e, 128), F32)],
        compiler_params=_params("arbitrary"),
        name="route_top2",
    )(logits_rows)


def _gather_norm_body(src_ref, used_ref, x_hbm, g_ref, o_ref, xbuf, sem, *, tb):
    i = pl.program_id(0)
    n_used = used_ref[0]

    def row_copy(src_row, slot, j):
        return pltpu.make_async_copy(x_hbm.at[pl.ds(src_row, 1)], xbuf.at[slot, pl.ds(j, 1)], sem.at[slot])

    def issue_tile(tile):
        slot = tile % 2

        def issue(j, carry):
            row_copy(src_ref[tile * tb + j], slot, j).start()
            return carry

        lax.fori_loop(0, tb, issue, 0)

    @pl.when(i == 0)
    def _():
        issue_tile(i)

    @pl.when(i + 1 < n_used)
    def _():
        issue_tile(i + 1)

    @pl.when(i < n_used)
    def _():
        slot = i % 2

        def drain(j, carry):
            row_copy(0, slot, 0).wait()
            return carry

        lax.fori_loop(0, tb, drain, 0)
        x = xbuf[slot]
        inv = lax.rsqrt(jnp.mean(x * x, axis=-1, keepdims=True) + EPS)
        o_ref[...] = (x * inv * g_ref[...]).astype(o_ref.dtype)

    @pl.when(i >= n_used)
    def _():
        o_ref[...] = jnp.zeros(o_ref.shape, o_ref.dtype)


def moe_gather_norm(x, gain, src_rows, n_used, tb):
    t, d = x.shape
    r = src_rows.shape[0]
    body = functools.partial(_gather_norm_body, tb=tb)
    return pl.pallas_call(
        body,
        grid_spec=pltpu.PrefetchScalarGridSpec(
            num_scalar_prefetch=2,
            grid=(r // tb,),
            in_specs=[pl.BlockSpec(memory_space=pl.ANY), pl.BlockSpec((1, d), lambda i, src, nu: (0, 0))],
            out_specs=pl.BlockSpec((tb, d), lambda i, src, nu: (i, 0)),
            scratch_shapes=[pltpu.VMEM((2, tb, d), F32), pltpu.SemaphoreType.DMA((2,))],
        ),
        out_shape=jax.ShapeDtypeStruct((r, d), BF16),
        compiler_params=_params("arbitrary"),
        name="moe_gather_norm",
    )(src_rows, n_used, x, gain.reshape(1, d))


def _new_weight_block(exp_ref):
    i = pl.program_id(1)
    return jnp.logical_or(i == 0, exp_ref[i] != exp_ref[jnp.maximum(i - 1, 0)])


def _gmm_swiglu_body(src_ref, exp_ref, used_ref, x_ref, wg_ref, wu_ref, o_ref, wgb_ref, wub_ref):
    del src_ref

    @pl.when(_new_weight_block(exp_ref))
    def _():
        wgb_ref[...] = wg_ref[...].astype(BF16)
        wub_ref[...] = wu_ref[...].astype(BF16)

    @pl.when(pl.program_id(1) < used_ref[0])
    def _():
        a = x_ref[...]
        g = jnp.dot(a, wgb_ref[...], preferred_element_type=F32)
        u = jnp.dot(a, wub_ref[...], preferred_element_type=F32)
        o_ref[...] = _silu_mul(g, u).astype(o_ref.dtype)

    @pl.when(pl.program_id(1) >= used_ref[0])
    def _():
        o_ref[...] = jnp.zeros(o_ref.shape, o_ref.dtype)


def grouped_swiglu(xs, w13, tile_src, tile_exp, n_used, bm, bn):
    r, d = xs.shape
    f = w13.shape[2] // 2
    nb = f // bn
    return pl.pallas_call(
        _gmm_swiglu_body,
        grid_spec=pltpu.PrefetchScalarGridSpec(
            num_scalar_prefetch=3,
            grid=(nb, r // bm),
            in_specs=[
                pl.BlockSpec((bm, d), lambda j, i, src, ex, nu: (src[i], 0)),
                pl.BlockSpec((None, d, bn), lambda j, i, src, ex, nu: (ex[i], 0, j)),
                pl.BlockSpec((None, d, bn), lambda j, i, src, ex, nu: (ex[i], 0, j + nb)),
            ],
            out_specs=pl.BlockSpec((bm, bn), lambda j, i, src, ex, nu: (i, j)),
            scratch_shapes=[pltpu.VMEM((d, bn), BF16), pltpu.VMEM((d, bn), BF16)],
        ),
        out_shape=jax.ShapeDtypeStruct((r, f), BF16),
        compiler_params=_params("arbitrary", "arbitrary"),
        name="grouped_swiglu",
    )(tile_src, tile_exp, n_used, xs, w13, w13)


def _gmm_body(src_ref, exp_ref, used_ref, a_ref, w_ref, o_ref, wb_ref):
    del src_ref

    @pl.when(_new_weight_block(exp_ref))
    def _():
        wb_ref[...] = w_ref[...].astype(BF16)

    @pl.when(pl.program_id(1) < used_ref[0])
    def _():
        o_ref[...] = jnp.dot(a_ref[...], wb_ref[...], preferred_element_type=F32).astype(o_ref.dtype)

    @pl.when(pl.program_id(1) >= used_ref[0])
    def _():
        o_ref[...] = jnp.zeros(o_ref.shape, o_ref.dtype)


def grouped_matmul(a, w, tile_src, tile_exp, n_used, bm, bn):
    r, k = a.shape
    n = w.shape[2]
    return pl.pallas_call(
        _gmm_body,
        grid_spec=pltpu.PrefetchScalarGridSpec(
            num_scalar_prefetch=3,
            grid=(n // bn, r // bm),
            in_specs=[
                pl.BlockSpec((bm, k), lambda j, i, src, ex, nu: (src[i], 0)),
                pl.BlockSpec((None, k, bn), lambda j, i, src, ex, nu: (ex[i], 0, j)),
            ],
            out_specs=pl.BlockSpec((bm, bn), lambda j, i, src, ex, nu: (i, j)),
            scratch_shapes=[pltpu.VMEM((k, bn), BF16)],
        ),
        out_shape=jax.ShapeDtypeStruct((r, n), F32),
        compiler_params=_params("arbitrary", "arbitrary"),
        name="grouped_matmul",
    )(tile_src, tile_exp, n_used, a, w)


def _combine_body(pos_ref, x_ref, w_ref, y_hbm, o_ref, ybuf, sem, *, tb, t_total):
    i = pl.program_id(0)

    def row_copy(src_row, slot, choice, j):
        return pltpu.make_async_copy(
            y_hbm.at[pl.ds(src_row, 1)], ybuf.at[slot, choice, pl.ds(j, 1)], sem.at[slot])

    def issue_tile(tile):
        slot = tile % 2

        def issue(j, carry):
            t = tile * tb + j
            row_copy(pos_ref[t], slot, 0, j).start()
            row_copy(pos_ref[t_total + t], slot, 1, j).start()
            return carry

        lax.fori_loop(0, tb, issue, 0)

    @pl.when(i == 0)
    def _():
        issue_tile(i)

    @pl.when(i + 1 < pl.num_programs(0))
    def _():
        issue_tile(i + 1)

    slot = i % 2

    def drain(j, carry):
        row_copy(0, slot, 0, 0).wait()
        row_copy(0, slot, 1, 0).wait()
        return carry

    lax.fori_loop(0, tb, drain, 0)
    w = w_ref[...]
    o_ref[...] = x_ref[...] + w[:, 0:1] * ybuf[slot, 0] + w[:, 1:2] * ybuf[slot, 1]


def moe_combine(x, y, pos_flat, w_cols, tb=256):
    t, d = x.shape
    body = functools.partial(_combine_body, tb=tb, t_total=t)
    return pl.pallas_call(
        body,
        grid_spec=pltpu.PrefetchScalarGridSpec(
            num_scalar_prefetch=1,
            grid=(t // tb,),
            in_specs=[
                pl.BlockSpec((tb, d), lambda i, pos: (i, 0)),
                pl.BlockSpec((tb, 2), lambda i, pos: (i, 0)),
                pl.BlockSpec(memory_space=pl.ANY),
            ],
            out_specs=pl.BlockSpec((tb, d), lambda i, pos: (i, 0)),
            scratch_shapes=[pltpu.VMEM((2, 2, tb, d), F32), pltpu.SemaphoreType.DMA((2,))],
        ),
        out_shape=jax.ShapeDtypeStruct((t, d), F32),
        compiler_params=_params("arbitrary"),
        name="moe_combine",
    )(pos_flat, x, w_cols, y)


def moe_plan(idx, rank, counts, bm, max_tiles):
    t = idx.shape[1]
    tiles = (counts + bm - 1) // bm
    tile_end = jnp.cumsum(tiles)
    row_off = (tile_end - tiles) * bm
    experts = jnp.arange(tiles.shape[0], dtype=idx.dtype)[:, None, None]
    row_base = jnp.sum(jnp.where(idx[None] == experts, row_off[:, None, None], 0), axis=0)
    pos = (row_base + rank).reshape(-1).astype(jnp.int32)
    tokens = jnp.tile(jnp.arange(t, dtype=jnp.int32), 2)
    src_rows = jnp.zeros((max_tiles * bm,), jnp.int32).at[pos].set(tokens)
    n_used = tile_end[-1]
    tile_src = jnp.minimum(jnp.arange(max_tiles, dtype=jnp.int32), n_used - 1)
    tile_exp = jnp.sum(tile_src[:, None] >= tile_end[None, :], axis=1).astype(jnp.int32)
    return pos, src_rows, tile_src, tile_exp, n_used.reshape(1).astype(jnp.int32)


def _rope_tables(seq):
    rows = seq // GRID_W
    row_ids = jnp.repeat(jnp.arange(rows, dtype=F32), GRID_W)
    col_ids = jnp.tile(jnp.arange(GRID_W, dtype=F32), rows)
    n_freq = ATT_HEAD_DIM // 4
    inv_freq = ROPE_THETA ** (-jnp.arange(n_freq, dtype=F32) / n_freq)
    ang = jnp.concatenate([row_ids[:, None] * inv_freq, col_ids[:, None] * inv_freq], axis=-1)
    cos, sin = jnp.cos(ang), jnp.sin(ang)
    cos_full = jnp.repeat(cos, 2, axis=-1)
    sin_signed = jnp.stack([-sin, sin], axis=-1).reshape(seq, ATT_HEAD_DIM)
    return cos_full, sin_signed


def _attention_layer(x, b, s, norm_g, w_in, q_gain, k_gain, w_out):
    t, d = x.shape
    hn = rmsnorm(x, norm_g, BF16)
    proj = matmul(hn, w_in, w_in.shape[1], F32, bm=1024, bn=512)
    cos_full, sin_signed = _rope_tables(s)
    q, kt, v = qk_norm_rope(proj, cos_full, sin_signed, q_gain, k_gain, b, s,
                            ATT_HEADS, ATT_KV_HEADS, ATT_HEAD_DIM)
    o = attention(q, kt, v, ATT_KV_HEADS, ATT_GROUP, ATT_HEAD_DIM)
    return matmul(o.reshape(t, -1), w_out, d, F32, bm=1024, bn=512, residual=x)


def _dense_ffn_layer(x, norm_g, w13, w2):
    hn = rmsnorm(x, norm_g, BF16)
    act = matmul_swiglu(hn, w13, bm=512, bn=512)
    return matmul_kgrid_res(act, w2, x, bm=1024, bn=1024, bk=2048)


def _mlstm_layer(x, b, s, norm_g, w_in, gate_bias, h_gain, w_out):
    t, d = x.shape
    n_main = 2 * ML_HEADS * ML_QK_DIM + 2 * ML_HEADS * ML_V_DIM
    hn = rmsnorm(x, norm_g, BF16)
    proj = matmul(hn, w_in, n_main, BF16, bm=1024, bn=512)
    pre = norm_matmul_hp(x, norm_g, w_in[:, n_main:])
    pre_rows = pre.reshape(b, s, 4 * ML_HEADS).transpose(0, 2, 1)
    grow = mlstm_gates(pre_rows, gate_bias, ML_HEADS, ML_CHUNK)
    gcol = grow.transpose(0, 2, 1)
    h_fwd, h_bwd = mlstm_scan(proj.reshape(b, s, n_main), gcol, grow, ML_HEADS, ML_QK_DIM, ML_V_DIM, ML_CHUNK)
    o_blk = (2 * ML_HEADS * ML_QK_DIM + ML_HEADS * ML_V_DIM) // (ML_HEADS * ML_V_DIM)
    gated = mlstm_out_gate(h_fwd.reshape(t, -1), h_bwd.reshape(t, -1), proj, h_gain, ML_HEADS, ML_V_DIM, o_blk)
    return matmul(gated, w_out, d, F32, bm=1024, bn=512, residual=x)


def _moe_layer(x, norm_g, router, w13, w2):
    t, d = x.shape
    logits = norm_matmul_hp(x, norm_g, router)
    idx, rank, w_rows, cnt = route_top2(logits.T)
    max_tiles = (2 * t) // MOE_BM + N_EXPERTS
    pos, src_rows, tile_src, tile_exp, n_used = moe_plan(idx, rank, cnt[:, 0].astype(jnp.int32), MOE_BM, max_tiles)
    xs = moe_gather_norm(x, norm_g, src_rows, n_used, MOE_BM)
    act = grouped_swiglu(xs, w13, tile_src, tile_exp, n_used, MOE_BM, 512)
    y = grouped_matmul(act, w2, tile_src, tile_exp, n_used, MOE_BM, 512)
    return moe_combine(x, y, pos, w_rows.T)


def kernel(x, norm_mix, norm_ffn, att_w_in, att_q_gain, att_k_gain, att_w_out, ffn_w13, ffn_w2,
           ml_w_in, ml_gate_bias, ml_h_gain, ml_w_out, moe_router, moe_w13, moe_w2):
    b, s, d = x.shape
    h = x.reshape(b * s, d)
    h = _attention_layer(h, b, s, norm_mix[0], att_w_in[0], att_q_gain[0], att_k_gain[0], att_w_out[0])
    h = _dense_ffn_layer(h, norm_ffn[0], ffn_w13[0], ffn_w2[0])
    h = _mlstm_layer(h, b, s, norm_mix[1], ml_w_in[0], ml_gate_bias[0], ml_h_gain[0], ml_w_out[0])
    h = _moe_layer(h, norm_ffn[1], moe_router[0], moe_w13[0], moe_w2[0])
    return h.reshape(b, s, d)
```

```python
import functools

import jax
import jax.numpy as jnp
from jax import lax
from jax.experimental import pallas as pl
from jax.experimental.pallas import tpu as pltpu

F32 = jnp.float32
BF16 = jnp.bfloat16

EPS = 1e-6
NEG = -1e30

GRID_W = 64
ROPE_THETA = 10000.0
ATT_HEADS = 32
ATT_KV_HEADS = 8
ATT_HEAD_DIM = 128
ATT_GROUP = ATT_HEADS // ATT_KV_HEADS

ML_HEADS = 8
ML_QK_DIM = 256
ML_V_DIM = 512
GATE_CAP = 15.0
ML_CHUNK = 256

N_EXPERTS = 8
MOE_BM = 512
ROW_DMA_UNROLL = 8

VMEM_LIMIT_BYTES = 56 * 1024 * 1024


def _params(*sem):
    return pltpu.CompilerParams(dimension_semantics=sem, vmem_limit_bytes=VMEM_LIMIT_BYTES)


def _rmsnorm_body(x_ref, g_ref, o_ref):
    x = x_ref[...]
    inv = lax.rsqrt(jnp.mean(x * x, axis=-1, keepdims=True) + EPS)
    o_ref[...] = (x * inv * g_ref[...]).astype(o_ref.dtype)


def rmsnorm(x, gain, out_dtype, bm=256):
    t, d = x.shape
    return pl.pallas_call(
        _rmsnorm_body,
        grid=(t // bm,),
        in_specs=[pl.BlockSpec((bm, d), lambda i: (i, 0)), pl.BlockSpec((1, d), lambda i: (0, 0))],
        out_specs=pl.BlockSpec((bm, d), lambda i: (i, 0)),
        out_shape=jax.ShapeDtypeStruct((t, d), out_dtype),
        compiler_params=_params("parallel"),
        name="rmsnorm",
    )(x, gain.reshape(1, d))


def _mm_body(a_ref, w_ref, o_ref, wb_ref):
    @pl.when(pl.program_id(1) == 0)
    def _():
        wb_ref[...] = w_ref[...].astype(BF16)

    o_ref[...] = jnp.dot(a_ref[...], wb_ref[...], preferred_element_type=F32).astype(o_ref.dtype)


def _mm_res_body(a_ref, w_ref, r_ref, o_ref, wb_ref):
    @pl.when(pl.program_id(1) == 0)
    def _():
        wb_ref[...] = w_ref[...].astype(BF16)

    o_ref[...] = r_ref[...] + jnp.dot(a_ref[...], wb_ref[...], preferred_element_type=F32)


def _mm_wt_body(a_ref, wt_ref, o_ref, wb_ref):
    @pl.when(pl.program_id(1) == 0)
    def _():
        wb_ref[...] = wt_ref[...].T.astype(BF16)

    o_ref[...] = jnp.dot(a_ref[...], wb_ref[...], preferred_element_type=F32).astype(o_ref.dtype)


def matmul(a, w, n, out_dtype, bm, bn, residual=None, w_transposed=False):
    m, k = a.shape
    in_specs = [pl.BlockSpec((bm, k), lambda j, i: (i, 0)), pl.BlockSpec((k, bn), lambda j, i: (0, j))]
    args = [a, w]
    body = _mm_body
    if w_transposed:
        assert residual is None
        in_specs[1] = pl.BlockSpec((bn, k), lambda j, i: (j, 0))
        body = _mm_wt_body
    if residual is not None:
        in_specs.append(pl.BlockSpec((bm, bn), lambda j, i: (i, j)))
        args.append(residual)
        body = _mm_res_body
    return pl.pallas_call(
        body,
        grid=(n // bn, m // bm),
        in_specs=in_specs,
        out_specs=pl.BlockSpec((bm, bn), lambda j, i: (i, j)),
        out_shape=jax.ShapeDtypeStruct((m, n), out_dtype),
        scratch_shapes=[pltpu.VMEM((k, bn), BF16)],
        compiler_params=_params("parallel", "arbitrary"),
        name="matmul",
    )(*args)


def _mm_kgrid_res_body(a_ref, w_ref, r_ref, o_ref):
    part = jnp.dot(a_ref[...], w_ref[...].astype(BF16), preferred_element_type=F32)

    @pl.when(pl.program_id(2) == 0)
    def _():
        o_ref[...] = r_ref[...] + part

    @pl.when(pl.program_id(2) != 0)
    def _():
        o_ref[...] += part


def matmul_kgrid_res(a, w, residual, bm, bn, bk):
    m, k = a.shape
    n = w.shape[1]
    return pl.pallas_call(
        _mm_kgrid_res_body,
        grid=(n // bn, m // bm, k // bk),
        in_specs=[
            pl.BlockSpec((bm, bk), lambda j, i, kk: (i, kk)),
            pl.BlockSpec((bk, bn), lambda j, i, kk: (kk, j)),
            pl.BlockSpec((bm, bn), lambda j, i, kk: (i, j)),
        ],
        out_specs=pl.BlockSpec((bm, bn), lambda j, i, kk: (i, j)),
        out_shape=jax.ShapeDtypeStruct((m, n), F32),
        compiler_params=_params("parallel", "parallel", "arbitrary"),
        name="matmul_kgrid_res",
    )(a, w, residual)


def _silu_mul(g, u):
    return g / (1.0 + jnp.exp(-g)) * u


def _mm_swiglu_body(a_ref, wg_ref, wu_ref, o_ref, wgb_ref, wub_ref):
    @pl.when(pl.program_id(1) == 0)
    def _():
        wgb_ref[...] = wg_ref[...].astype(BF16)
        wub_ref[...] = wu_ref[...].astype(BF16)

    a = a_ref[...]
    g = jnp.dot(a, wgb_ref[...], preferred_element_type=F32)
    u = jnp.dot(a, wub_ref[...], preferred_element_type=F32)
    o_ref[...] = _silu_mul(g, u).astype(o_ref.dtype)


def matmul_swiglu(a, w13, bm, bn):
    m, k = a.shape
    f = w13.shape[1] // 2
    nb = f // bn
    return pl.pallas_call(
        _mm_swiglu_body,
        grid=(nb, m // bm),
        in_specs=[
            pl.BlockSpec((bm, k), lambda j, i: (i, 0)),
            pl.BlockSpec((k, bn), lambda j, i: (0, j)),
            pl.BlockSpec((k, bn), lambda j, i: (0, j + nb)),
        ],
        out_specs=pl.BlockSpec((bm, bn), lambda j, i: (i, j)),
        out_shape=jax.ShapeDtypeStruct((m, f), BF16),
        scratch_shapes=[pltpu.VMEM((k, bn), BF16), pltpu.VMEM((k, bn), BF16)],
        compiler_params=_params("parallel", "arbitrary"),
        name="matmul_swiglu",
    )(a, w13, w13)


def _split_bf16(x):
    hi = x.astype(BF16)
    lo = (x - hi.astype(F32)).astype(BF16)
    return hi, lo


def _norm_mm_hp_body(x_ref, g_ref, w_ref, o_ref, *, w_transposed):
    x = x_ref[...]
    inv = lax.rsqrt(jnp.mean(x * x, axis=-1, keepdims=True) + EPS)
    xn = x * inv * g_ref[...]
    xh, xl = _split_bf16(xn)
    wh, wl = _split_bf16(w_ref[...])
    contract = (((1,), (1 if w_transposed else 0,)), ((), ()))
    acc = lax.dot_general(xh, wh, contract, preferred_element_type=F32)
    acc += lax.dot_general(xh, wl, contract, preferred_element_type=F32)
    acc += lax.dot_general(xl, wh, contract, preferred_element_type=F32)
    o_ref[...] = acc


def norm_matmul_hp(x, gain, w, bm=512, wt_rows=None):
    t, d = x.shape
    if wt_rows is None:
        n = w.shape[1]
        w_spec = pl.BlockSpec((d, n), lambda i: (0, 0))
    else:
        r0, n = wt_rows
        assert r0 % n == 0
        w_spec = pl.BlockSpec((n, d), lambda i: (r0 // n, 0))
    return pl.pallas_call(
        functools.partial(_norm_mm_hp_body, w_transposed=wt_rows is not None),
        grid=(t // bm,),
        in_specs=[
            pl.BlockSpec((bm, d), lambda i: (i, 0)),
            pl.BlockSpec((1, d), lambda i: (0, 0)),
            w_spec,
        ],
        out_specs=pl.BlockSpec((bm, n), lambda i: (i, 0)),
        out_shape=jax.ShapeDtypeStruct((t, n), F32),
        compiler_params=_params("parallel"),
        name="norm_matmul_hp",
    )(x, gain.reshape(1, d), w)


def _qk_rope_body(p_ref, cos_ref, sin_ref, qg_ref, kg_ref, q_ref, kt_ref, v_ref, *, n_q, n_kv, dh, q_scale):
    cos = cos_ref[...]
    sin = sin_ref[...]
    pr = lax.broadcasted_iota(jnp.int32, (dh, dh), 0)
    pc = lax.broadcasted_iota(jnp.int32, (dh, dh), 1)
    swap = jnp.where(jnp.bitwise_xor(pr, 1) == pc, 1.0, 0.0).astype(BF16)

    def norm_rope(x, gain):
        xn = x * lax.rsqrt(jnp.mean(x * x, axis=-1, keepdims=True) + EPS) * gain
        partner = jnp.dot(xn.astype(BF16), swap, preferred_element_type=F32)
        return xn * cos + partner * sin

    for h in range(n_q):
        x = p_ref[:, h * dh:(h + 1) * dh]
        q_ref[:, h * dh:(h + 1) * dh] = (norm_rope(x, qg_ref[...]) * q_scale).astype(q_ref.dtype)
    for h in range(n_kv):
        x = p_ref[:, (n_q + h) * dh:(n_q + h + 1) * dh]
        kt_ref[h * dh:(h + 1) * dh, :] = norm_rope(x, kg_ref[...]).T.astype(kt_ref.dtype)
    v0 = (n_q + n_kv) * dh
    v_ref[...] = p_ref[:, v0:v0 + n_kv * dh].astype(v_ref.dtype)


def qk_norm_rope(proj, cos_full, sin_signed, q_gain, k_gain, batch, seq, n_q, n_kv, dh, bm=256):
    sb = seq // bm
    body = functools.partial(_qk_rope_body, n_q=n_q, n_kv=n_kv, dh=dh, q_scale=dh ** -0.5)
    return pl.pallas_call(
        body,
        grid=(batch, sb),
        in_specs=[
            pl.BlockSpec((bm, proj.shape[1]), lambda b, i: (b * sb + i, 0)),
            pl.BlockSpec((bm, dh), lambda b, i: (i, 0)),
            pl.BlockSpec((bm, dh), lambda b, i: (i, 0)),
            pl.BlockSpec((1, dh), lambda b, i: (0, 0)),
            pl.BlockSpec((1, dh), lambda b, i: (0, 0)),
        ],
        out_specs=[
            pl.BlockSpec((None, bm, n_q * dh), lambda b, i: (b, i, 0)),
            pl.BlockSpec((None, n_kv * dh, bm), lambda b, i: (b, 0, i)),
            pl.BlockSpec((None, bm, n_kv * dh), lambda b, i: (b, i, 0)),
        ],
        out_shape=[
            jax.ShapeDtypeStruct((batch, seq, n_q * dh), BF16),
            jax.ShapeDtypeStruct((batch, n_kv * dh, seq), BF16),
            jax.ShapeDtypeStruct((batch, seq, n_kv * dh), BF16),
        ],
        compiler_params=_params("parallel", "parallel"),
        name="qk_norm_rope",
    )(proj, cos_full, sin_signed, q_gain.reshape(1, dh), k_gain.reshape(1, dh))


def _attn_body(q_ref, kt_ref, v_ref, o_ref, vx_ref, *, group, dh):
    @pl.when(pl.program_id(2) == 0)
    def _():
        vx_ref[:, 0:dh] = v_ref[...]
        vx_ref[:, dh:2 * dh] = jnp.ones((v_ref.shape[0], dh), BF16)

    for g in range(group):
        q = q_ref[:, g * dh:(g + 1) * dh]
        s = jnp.dot(q, kt_ref[...], preferred_element_type=F32)
        m = jnp.max(s, axis=-1, keepdims=True)
        p = jnp.exp(s - m).astype(BF16)
        ox = jnp.dot(p, vx_ref[...], preferred_element_type=F32)
        o = ox[:, 0:dh] * (1.0 / ox[:, dh:dh + 1])
        o_ref[:, g * dh:(g + 1) * dh] = o.astype(o_ref.dtype)


def attention(q, kt, v, n_kv, group, dh, tq=512):
    b, s, _ = q.shape
    body = functools.partial(_attn_body, group=group, dh=dh)
    return pl.pallas_call(
        body,
        grid=(b, n_kv, s // tq),
        in_specs=[
            pl.BlockSpec((None, tq, group * dh), lambda bi, kv, i: (bi, i, kv)),
            pl.BlockSpec((None, dh, s), lambda bi, kv, i: (bi, kv, 0)),
            pl.BlockSpec((None, s, dh), lambda bi, kv, i: (bi, 0, kv)),
        ],
        out_specs=pl.BlockSpec((None, tq, group * dh), lambda bi, kv, i: (bi, i, kv)),
        out_shape=jax.ShapeDtypeStruct(q.shape, BF16),
        scratch_shapes=[pltpu.VMEM((s, 2 * dh), BF16)],
        compiler_params=_params("parallel", "parallel", "arbitrary"),
        name="attention",
    )(q, kt, v)


def _cumsum_dot(x, tri):
    hi = x.astype(BF16)
    r1 = x - hi.astype(F32)
    mid = r1.astype(BF16)
    lo = (r1 - mid.astype(F32)).astype(BF16)
    acc = jnp.dot(hi, tri, preferred_element_type=F32)
    acc += jnp.dot(mid, tri, preferred_element_type=F32)
    acc += jnp.dot(lo, tri, preferred_element_type=F32)
    return acc


def _gates_body(pre_ref, bias_ref, o_ref, *, heads, chunk_shift):
    g = pre_ref[...] + bias_ref[...]
    g = GATE_CAP * jnp.tanh(g / GATE_CAP)
    logsig = jnp.minimum(g, 0.0) - jnp.log1p(jnp.exp(-jnp.abs(g)))
    n = g.shape[1]
    t = lax.broadcasted_iota(jnp.int32, (n, n), 0)
    j = lax.broadcasted_iota(jnp.int32, (n, n), 1)
    same = lax.shift_right_logical(t, chunk_shift) == lax.shift_right_logical(j, chunk_shift)
    tri_f = jnp.where(jnp.logical_and(same, t <= j), 1.0, 0.0).astype(BF16)
    tri_b = jnp.where(jnp.logical_and(same, t >= j), 1.0, 0.0).astype(BF16)
    h = heads
    bc_f = _cumsum_dot(logsig[h:2 * h], tri_f)
    suf_b = _cumsum_dot(logsig[3 * h:4 * h], tri_b)
    o_ref[0:h, :] = bc_f
    o_ref[h:2 * h, :] = g[0:h] - bc_f
    o_ref[2 * h:3 * h, :] = suf_b
    o_ref[3 * h:4 * h, :] = g[2 * h:3 * h] - suf_b


def mlstm_gates(pre_rows, bias, heads, chunk, sb=512):
    b, r, s = pre_rows.shape
    body = functools.partial(_gates_body, heads=heads, chunk_shift=chunk.bit_length() - 1)
    return pl.pallas_call(
        body,
        grid=(b, s // sb),
        in_specs=[
            pl.BlockSpec((None, r, sb), lambda bi, i: (bi, 0, i)),
            pl.BlockSpec((r, 1), lambda bi, i: (0, 0)),
        ],
        out_specs=pl.BlockSpec((None, r, sb), lambda bi, i: (bi, 0, i)),
        out_shape=jax.ShapeDtypeStruct((b, r, s), F32),
        compiler_params=_params("parallel", "parallel"),
        name="mlstm_gates",
    )(pre_rows, bias.reshape(r, 1))


def _mlstm_chunk(q_ref, k_ref, v_ref, gcol_ref, grow_ref, h_ref, c_ref, n_ref, m_ref, hd, *, heads, q_scale, backward):
    q = q_ref[...]
    k = k_ref[...]
    v = v_ref[...]
    ln = q.shape[0]
    gcol = gcol_ref[...]
    lane = lax.broadcasted_iota(jnp.int32, gcol.shape, 1)
    base = (2 * heads if backward else 0) + hd
    bcol = jnp.sum(jnp.where(lane == base, gcol, 0.0), axis=1, keepdims=True)
    rcol = jnp.sum(jnp.where(lane == base + heads, gcol, 0.0), axis=1, keepdims=True)
    rrow = grow_ref[pl.ds(base + heads, 1), :]

    row_i = lax.broadcasted_iota(jnp.int32, (ln, ln), 0)
    col_i = lax.broadcasted_iota(jnp.int32, (ln, ln), 1)
    mask = (col_i >= row_i) if backward else (col_i <= row_i)
    dmat = jnp.where(mask, bcol + rrow, NEG)
    m_prev = m_ref[...]
    inter = bcol + m_prev
    m_row = jnp.maximum(jnp.max(dmat, axis=1, keepdims=True), inter)
    qk = lax.dot_general(q, k, (((1,), (1,)), ((), ())), preferred_element_type=F32) * q_scale
    s = qk * jnp.exp(dmat - m_row)
    decay = jnp.exp(inter - m_row)
    q_c = jnp.dot(q, c_ref[...].astype(BF16), preferred_element_type=F32) * q_scale
    num = jnp.dot(s.astype(BF16), v, preferred_element_type=F32) + decay * q_c
    q_n = jnp.sum(q.astype(F32) * n_ref[...], axis=1, keepdims=True) * q_scale
    den = jnp.sum(s, axis=1, keepdims=True) + decay * q_n
    h_ref[...] = num / jnp.maximum(jnp.abs(den), jnp.exp(-m_row))

    g_last = bcol[0:1, :] if backward else bcol[ln - 1:ln, :]
    wcol = g_last + rcol
    m_new = jnp.maximum(g_last + m_prev, jnp.max(wcol, axis=0, keepdims=True))
    carry_decay = jnp.exp(g_last + m_prev - m_new)
    wk = jnp.exp(wcol - m_new) * k.astype(F32)
    kv = lax.dot_general(wk.astype(BF16), v, (((0,), (0,)), ((), ())), preferred_element_type=F32)
    c_ref[...] = carry_decay * c_ref[...] + kv
    n_ref[...] = carry_decay * n_ref[...] + jnp.sum(wk, axis=0, keepdims=True)
    m_ref[...] = m_new


def _mlstm_body(qf_ref, kf_ref, vf_ref, gcf_ref, grf_ref, qb_ref, kb_ref, vb_ref, gcb_ref, grb_ref,
                hf_ref, hb_ref, c_ref, n_ref, m_ref, *, heads, q_scale):
    hd = pl.program_id(1)

    @pl.when(pl.program_id(2) == 0)
    def _():
        c_ref[...] = jnp.zeros(c_ref.shape, F32)
        n_ref[...] = jnp.zeros(n_ref.shape, F32)
        m_ref[...] = jnp.full(m_ref.shape, NEG, F32)

    chunk = functools.partial(_mlstm_chunk, heads=heads, q_scale=q_scale)
    chunk(qf_ref, kf_ref, vf_ref, gcf_ref, grf_ref, hf_ref, c_ref.at[0], n_ref.at[0], m_ref.at[0], hd,
          backward=False)
    chunk(qb_ref, kb_ref, vb_ref, gcb_ref, grb_ref, hb_ref, c_ref.at[1], n_ref.at[1], m_ref.at[1], hd,
          backward=True)


def mlstm_scan(proj, gcol, grow, heads, dk, dv, chunk):
    b, s, _ = proj.shape
    nc = s // chunk
    body = functools.partial(_mlstm_body, heads=heads, q_scale=dk ** -0.5)
    k_blk0 = heads
    v_blk0 = (2 * heads * dk) // dv

    def direction_specs(cidx):
        return [
            pl.BlockSpec((None, chunk, dk), lambda bi, h, c: (bi, cidx(c), h)),
            pl.BlockSpec((None, chunk, dk), lambda bi, h, c: (bi, cidx(c), k_blk0 + h)),
            pl.BlockSpec((None, chunk, dv), lambda bi, h, c: (bi, cidx(c), v_blk0 + h)),
            pl.BlockSpec((None, chunk, 4 * heads), lambda bi, h, c: (bi, cidx(c), 0)),
            pl.BlockSpec((None, 4 * heads, chunk), lambda bi, h, c: (bi, 0, cidx(c))),
        ]

    def fwd(c):
        return c

    def bwd(c):
        return nc - 1 - c

    out_sds = jax.ShapeDtypeStruct((b, s, heads * dv), F32)
    return pl.pallas_call(
        body,
        grid=(b, heads, nc),
        in_specs=direction_specs(fwd) + direction_specs(bwd),
        out_specs=[
            pl.BlockSpec((None, chunk, dv), lambda bi, h, c: (bi, fwd(c), h)),
            pl.BlockSpec((None, chunk, dv), lambda bi, h, c: (bi, bwd(c), h)),
        ],
        out_shape=[out_sds, out_sds],
        scratch_shapes=[
            pltpu.VMEM((2, dk, dv), F32),
            pltpu.VMEM((2, 1, dk), F32),
            pltpu.VMEM((2, 1, 1), F32),
        ],
        compiler_params=_params("parallel", "parallel", "arbitrary"),
        name="mlstm_scan",
    )(*([proj, proj, proj, gcol, grow] * 2))


def _ml_out_body(hf_ref, hb_ref, o_ref, g_ref, out_ref, *, heads, dv):
    for h in range(heads):
        sl = slice(h * dv, (h + 1) * dv)
        x = hf_ref[:, sl] + hb_ref[:, sl]
        xn = x * lax.rsqrt(jnp.mean(x * x, axis=-1, keepdims=True) + EPS) * g_ref[:, sl]
        og = o_ref[:, sl].astype(F32)
        out_ref[:, sl] = (xn / (1.0 + jnp.exp(-og))).astype(out_ref.dtype)


def mlstm_out_gate(h_fwd, h_bwd, proj, h_gain, heads, dv, o_blk, bm=256):
    t, d = h_fwd.shape
    body = functools.partial(_ml_out_body, heads=heads, dv=dv)
    return pl.pallas_call(
        body,
        grid=(t // bm,),
        in_specs=[
            pl.BlockSpec((bm, d), lambda i: (i, 0)),
            pl.BlockSpec((bm, d), lambda i: (i, 0)),
            pl.BlockSpec((bm, d), lambda i: (i, o_blk)),
            pl.BlockSpec((1, d), lambda i: (0, 0)),
        ],
        out_specs=pl.BlockSpec((bm, d), lambda i: (i, 0)),
        out_shape=jax.ShapeDtypeStruct((t, d), BF16),
        compiler_params=_params("parallel"),
        name="mlstm_out_gate",
    )(h_fwd, h_bwd, proj, h_gain.reshape(1, d))


def _route_body(lg_ref, idx_ref, rank_ref, w_ref, cnt_ref, carry_ref):
    @pl.when(pl.program_id(0) == 0)
    def _():
        carry_ref[...] = jnp.zeros(carry_ref.shape, F32)

    lg = lg_ref[...]
    n_e, tb = lg.shape
    e_iota = lax.broadcasted_iota(jnp.int32, lg.shape, 0).astype(F32)
    t1 = jnp.max(lg, axis=0, keepdims=True)
    i1 = jnp.min(jnp.where(lg == t1, e_iota, float(n_e)), axis=0, keepdims=True)
    first = e_iota == i1
    lg2 = jnp.where(first, -jnp.inf, lg)
    t2 = jnp.max(lg2, axis=0, keepdims=True)
    i2 = jnp.min(jnp.where(lg2 == t2, e_iota, float(n_e)), axis=0, keepdims=True)
    second = e_iota == i2
    e2 = jnp.exp(t2 - t1)
    w_ref[0:1, :] = 1.0 / (1.0 + e2)
    w_ref[1:2, :] = e2 / (1.0 + e2)
    idx_ref[0:1, :] = i1.astype(jnp.int32)
    idx_ref[1:2, :] = i2.astype(jnp.int32)

    assign = jnp.where(first, 1.0, 0.0) + jnp.where(second, 1.0, 0.0)
    tp = lax.broadcasted_iota(jnp.int32, (tb, tb), 0)
    tc = lax.broadcasted_iota(jnp.int32, (tb, tb), 1)
    before = jnp.where(tp < tc, 1.0, 0.0).astype(BF16)
    rank = jnp.dot(assign.astype(BF16), before, preferred_element_type=F32) + carry_ref[:, 0:1]
    rank_ref[0:1, :] = jnp.sum(jnp.where(first, rank, 0.0), axis=0, keepdims=True).astype(jnp.int32)
    rank_ref[1:2, :] = jnp.sum(jnp.where(second, rank, 0.0), axis=0, keepdims=True).astype(jnp.int32)
    carry_ref[...] = carry_ref[...] + jnp.sum(assign, axis=1, keepdims=True)
    cnt_ref[...] = carry_ref[...]


def route_top2(logits_rows, tb=512):
    n_e, t = logits_rows.shape
    return pl.pallas_call(
        _route_body,
        grid=(t // tb,),
        in_specs=[pl.BlockSpec((n_e, tb), lambda i: (0, i))],
        out_specs=[
            pl.BlockSpec((2, tb), lambda i: (0, i)),
            pl.BlockSpec((2, tb), lambda i: (0, i)),
            pl.BlockSpec((2, tb), lambda i: (0, i)),
            pl.BlockSpec((n_e, 128), lambda i: (0, 0)),
        ],
        out_shape=[
            jax.ShapeDtypeStruct((2, t), jnp.int32),
            jax.ShapeDtypeStruct((2, t), jnp.int32),
            jax.ShapeDtypeStruct((2, t), F32),
            jax.ShapeDtypeStruct((n_e, 128), F32),
        ],
        scratch_shapes=[pltpu.VMEM((n_e, 128), F32)],
        compiler_params=_params("arbitrary"),
        name="route_top2",
    )(logits_rows)


def _gather_norm_body(src_ref, used_ref, x_hbm, g_ref, o_ref, xbuf, sem, *, tb):
    i = pl.program_id(0)
    n_used = used_ref[0]

    def row_copy(src_row, slot, j):
        return pltpu.make_async_copy(x_hbm.at[pl.ds(src_row, 1)], xbuf.at[slot, pl.ds(j, 1)], sem.at[slot])

    def issue_tile(tile):
        slot = tile % 2

        def issue(j, carry):
            row_copy(src_ref[tile * tb + j], slot, j).start()
            return carry

        lax.fori_loop(0, tb, issue, 0, unroll=ROW_DMA_UNROLL)

    @pl.when(i == 0)
    def _():
        issue_tile(i)

    @pl.when(i + 1 < n_used)
    def _():
        issue_tile(i + 1)

    @pl.when(i < n_used)
    def _():
        slot = i % 2
        pltpu.make_async_copy(x_hbm.at[pl.ds(0, tb)], xbuf.at[slot], sem.at[slot]).wait()
        x = xbuf[slot]
        inv = lax.rsqrt(jnp.mean(x * x, axis=-1, keepdims=True) + EPS)
        o_ref[...] = (x * inv * g_ref[...]).astype(o_ref.dtype)

    @pl.when(i >= n_used)
    def _():
        o_ref[...] = jnp.zeros(o_ref.shape, o_ref.dtype)


def moe_gather_norm(x, gain, src_rows, n_used, tb):
    t, d = x.shape
    r = src_rows.shape[0]
    body = functools.partial(_gather_norm_body, tb=tb)
    return pl.pallas_call(
        body,
        grid_spec=pltpu.PrefetchScalarGridSpec(
            num_scalar_prefetch=2,
            grid=(r // tb,),
            in_specs=[pl.BlockSpec(memory_space=pl.ANY), pl.BlockSpec((1, d), lambda i, src, nu: (0, 0))],
            out_specs=pl.BlockSpec((tb, d), lambda i, src, nu: (i, 0)),
            scratch_shapes=[pltpu.VMEM((2, tb, d), F32), pltpu.SemaphoreType.DMA((2,))],
        ),
        out_shape=jax.ShapeDtypeStruct((r, d), BF16),
        compiler_params=_params("arbitrary"),
        name="moe_gather_norm",
    )(src_rows, n_used, x, gain.reshape(1, d))


def _new_weight_block(exp_ref):
    i = pl.program_id(1)
    return jnp.logical_or(i == 0, exp_ref[i] != exp_ref[jnp.maximum(i - 1, 0)])


def _gmm_swiglu_body(src_ref, exp_ref, used_ref, x_ref, wg_ref, wu_ref, o_ref, wgb_ref, wub_ref):
    del src_ref

    @pl.when(_new_weight_block(exp_ref))
    def _():
        wgb_ref[...] = wg_ref[...].astype(BF16)
        wub_ref[...] = wu_ref[...].astype(BF16)

    @pl.when(pl.program_id(1) < used_ref[0])
    def _():
        a = x_ref[...]
        g = jnp.dot(a, wgb_ref[...], preferred_element_type=F32)
        u = jnp.dot(a, wub_ref[...], preferred_element_type=F32)
        o_ref[...] = _silu_mul(g, u).astype(o_ref.dtype)

    @pl.when(pl.program_id(1) >= used_ref[0])
    def _():
        o_ref[...] = jnp.zeros(o_ref.shape, o_ref.dtype)


def grouped_swiglu(xs, w13, tile_src, tile_exp, n_used, bm, bn):
    r, d = xs.shape
    f = w13.shape[2] // 2
    nb = f // bn
    return pl.pallas_call(
        _gmm_swiglu_body,
        grid_spec=pltpu.PrefetchScalarGridSpec(
            num_scalar_prefetch=3,
            grid=(nb, r // bm),
            in_specs=[
                pl.BlockSpec((bm, d), lambda j, i, src, ex, nu: (src[i], 0)),
                pl.BlockSpec((None, d, bn), lambda j, i, src, ex, nu: (ex[i], 0, j)),
                pl.BlockSpec((None, d, bn), lambda j, i, src, ex, nu: (ex[i], 0, j + nb)),
            ],
            out_specs=pl.BlockSpec((bm, bn), lambda j, i, src, ex, nu: (i, j)),
            scratch_shapes=[pltpu.VMEM((d, bn), BF16), pltpu.VMEM((d, bn), BF16)],
        ),
        out_shape=jax.ShapeDtypeStruct((r, f), BF16),
        compiler_params=_params("arbitrary", "arbitrary"),
        name="grouped_swiglu",
    )(tile_src, tile_exp, n_used, xs, w13, w13)


def _gmm_body(src_ref, exp_ref, used_ref, a_ref, w_ref, o_ref, wb_ref):
    del src_ref

    @pl.when(_new_weight_block(exp_ref))
    def _():
        wb_ref[...] = w_ref[...].astype(BF16)

    @pl.when(pl.program_id(1) < used_ref[0])
    def _():
        o_ref[...] = jnp.dot(a_ref[...], wb_ref[...], preferred_element_type=F32).astype(o_ref.dtype)

    @pl.when(pl.program_id(1) >= used_ref[0])
    def _():
        o_ref[...] = jnp.zeros(o_ref.shape, o_ref.dtype)


def grouped_matmul(a, w, tile_src, tile_exp, n_used, bm, bn):
    r, k = a.shape
    n = w.shape[2]
    return pl.pallas_call(
        _gmm_body,
        grid_spec=pltpu.PrefetchScalarGridSpec(
            num_scalar_prefetch=3,
            grid=(n // bn, r // bm),
            in_specs=[
                pl.BlockSpec((bm, k), lambda j, i, src, ex, nu: (src[i], 0)),
                pl.BlockSpec((None, k, bn), lambda j, i, src, ex, nu: (ex[i], 0, j)),
            ],
            out_specs=pl.BlockSpec((bm, bn), lambda j, i, src, ex, nu: (i, j)),
            scratch_shapes=[pltpu.VMEM((k, bn), BF16)],
        ),
        out_shape=jax.ShapeDtypeStruct((r, n), F32),
        compiler_params=_params("arbitrary", "arbitrary"),
        name="grouped_matmul",
    )(tile_src, tile_exp, n_used, a, w)


def _combine_body(pos_ref, x_ref, w_ref, y_hbm, o_ref, ybuf, sem, *, tb, t_total):
    i = pl.program_id(0)

    def row_copy(src_row, slot, choice, j):
        return pltpu.make_async_copy(
            y_hbm.at[pl.ds(src_row, 1)], ybuf.at[slot, choice, pl.ds(j, 1)], sem.at[slot])

    def issue_tile(tile):
        slot = tile % 2

        def issue(j, carry):
            t = tile * tb + j
            row_copy(pos_ref[t], slot, 0, j).start()
            row_copy(pos_ref[t_total + t], slot, 1, j).start()
            return carry

        lax.fori_loop(0, tb, issue, 0, unroll=ROW_DMA_UNROLL)

    @pl.when(i == 0)
    def _():
        issue_tile(i)

    @pl.when(i + 1 < pl.num_programs(0))
    def _():
        issue_tile(i + 1)

    slot = i % 2
    for choice in range(2):
        pltpu.make_async_copy(y_hbm.at[pl.ds(0, tb)], ybuf.at[slot, choice], sem.at[slot]).wait()
    w = w_ref[...]
    o_ref[...] = x_ref[...] + w[:, 0:1] * ybuf[slot, 0] + w[:, 1:2] * ybuf[slot, 1]


def moe_combine(x, y, pos_flat, w_cols, tb=256):
    t, d = x.shape
    body = functools.partial(_combine_body, tb=tb, t_total=t)
    return pl.pallas_call(
        body,
        grid_spec=pltpu.PrefetchScalarGridSpec(
            num_scalar_prefetch=1,
            grid=(t // tb,),
            in_specs=[
                pl.BlockSpec((tb, d), lambda i, pos: (i, 0)),
                pl.BlockSpec((tb, 2), lambda i, pos: (i, 0)),
                pl.BlockSpec(memory_space=pl.ANY),
            ],
            out_specs=pl.BlockSpec((tb, d), lambda i, pos: (i, 0)),
            scratch_shapes=[pltpu.VMEM((2, 2, tb, d), F32), pltpu.SemaphoreType.DMA((2,))],
        ),
        out_shape=jax.ShapeDtypeStruct((t, d), F32),
        compiler_params=_params("arbitrary"),
        name="moe_combine",
    )(pos_flat, x, w_cols, y)


def moe_plan(idx, rank, counts, bm, max_tiles):
    t = idx.shape[1]
    tiles = (counts + bm - 1) // bm
    tile_end = jnp.cumsum(tiles)
    row_off = (tile_end - tiles) * bm
    experts = jnp.arange(tiles.shape[0], dtype=idx.dtype)[:, None, None]
    row_base = jnp.sum(jnp.where(idx[None] == experts, row_off[:, None, None], 0), axis=0)
    pos = (row_base + rank).reshape(-1).astype(jnp.int32)
    tokens = jnp.tile(jnp.arange(t, dtype=jnp.int32), 2)
    src_rows = jnp.zeros((max_tiles * bm,), jnp.int32).at[pos].set(tokens)
    n_used = tile_end[-1]
    tile_src = jnp.minimum(jnp.arange(max_tiles, dtype=jnp.int32), n_used - 1)
    tile_exp = jnp.sum(tile_src[:, None] >= tile_end[None, :], axis=1).astype(jnp.int32)
    return pos, src_rows, tile_src, tile_exp, n_used.reshape(1).astype(jnp.int32)


def _rope_tables(seq):
    rows = seq // GRID_W
    row_ids = jnp.repeat(jnp.arange(rows, dtype=F32), GRID_W)
    col_ids = jnp.tile(jnp.arange(GRID_W, dtype=F32), rows)
    n_freq = ATT_HEAD_DIM // 4
    inv_freq = ROPE_THETA ** (-jnp.arange(n_freq, dtype=F32) / n_freq)
    ang = jnp.concatenate([row_ids[:, None] * inv_freq, col_ids[:, None] * inv_freq], axis=-1)
    cos, sin = jnp.cos(ang), jnp.sin(ang)
    cos_full = jnp.repeat(cos, 2, axis=-1)
    sin_signed = jnp.stack([-sin, sin], axis=-1).reshape(seq, ATT_HEAD_DIM)
    return cos_full, sin_signed


def _attention_layer(x, b, s, norm_g, w_in, q_gain, k_gain, w_out):
    t, d = x.shape
    hn = rmsnorm(x, norm_g, BF16)
    proj = matmul(hn, w_in, w_in.shape[1], F32, bm=1024, bn=512)
    cos_full, sin_signed = _rope_tables(s)
    q, kt, v = qk_norm_rope(proj, cos_full, sin_signed, q_gain, k_gain, b, s,
                            ATT_HEADS, ATT_KV_HEADS, ATT_HEAD_DIM)
    o = attention(q, kt, v, ATT_KV_HEADS, ATT_GROUP, ATT_HEAD_DIM)
    return matmul(o.reshape(t, -1), w_out, d, F32, bm=1024, bn=512, residual=x)


def _dense_ffn_layer(x, norm_g, w13, w2):
    hn = rmsnorm(x, norm_g, BF16)
    act = matmul_swiglu(hn, w13, bm=512, bn=512)
    return matmul_kgrid_res(act, w2, x, bm=1024, bn=1024, bk=2048)


def _mlstm_layer(x, b, s, norm_g, w_in, gate_bias, h_gain, w_out):
    t, d = x.shape
    n_main = 2 * ML_HEADS * ML_QK_DIM + 2 * ML_HEADS * ML_V_DIM
    hn = rmsnorm(x, norm_g, BF16)
    w_in_t = w_in.T
    proj = matmul(hn, w_in_t, n_main, BF16, bm=1024, bn=512, w_transposed=True)
    pre = norm_matmul_hp(x, norm_g, w_in_t, wt_rows=(n_main, 4 * ML_HEADS))
    pre_rows = pre.reshape(b, s, 4 * ML_HEADS).transpose(0, 2, 1)
    grow = mlstm_gates(pre_rows, gate_bias, ML_HEADS, ML_CHUNK)
    gcol = grow.transpose(0, 2, 1)
    h_fwd, h_bwd = mlstm_scan(proj.reshape(b, s, n_main), gcol, grow, ML_HEADS, ML_QK_DIM, ML_V_DIM, ML_CHUNK)
    o_blk = (2 * ML_HEADS * ML_QK_DIM + ML_HEADS * ML_V_DIM) // (ML_HEADS * ML_V_DIM)
    gated = mlstm_out_gate(h_fwd.reshape(t, -1), h_bwd.reshape(t, -1), proj, h_gain, ML_HEADS, ML_V_DIM, o_blk)
    return matmul(gated, w_out, d, F32, bm=1024, bn=512, residual=x)


def _moe_layer(x, norm_g, router, w13, w2):
    t, d = x.shape
    logits = norm_matmul_hp(x, norm_g, router)
    idx, rank, w_rows, cnt = route_top2(logits.T)
    max_tiles = (2 * t) // MOE_BM + N_EXPERTS
    pos, src_rows, tile_src, tile_exp, n_used = moe_plan(idx, rank, cnt[:, 0].astype(jnp.int32), MOE_BM, max_tiles)
    xs = moe_gather_norm(x, norm_g, src_rows, n_used, MOE_BM)
    act = grouped_swiglu(xs, w13, tile_src, tile_exp, n_used, MOE_BM, 512)
    y = grouped_matmul(act, w2, tile_src, tile_exp, n_used, MOE_BM, 512)
    return moe_combine(x, y, pos, w_rows.T)


def kernel(x, norm_mix, norm_ffn, att_w_in, att_q_gain, att_k_gain, att_w_out, ffn_w13, ffn_w2,
           ml_w_in, ml_gate_bias, ml_h_gain, ml_w_out, moe_router, moe_w13, moe_w2):
    b, s, d = x.shape
    h = x.reshape(b * s, d)
    h = _attention_layer(h, b, s, norm_mix[0], att_w_in[0], att_q_gain[0], att_k_gain[0], att_w_out[0])
    h = _dense_ffn_layer(h, norm_ffn[0], ffn_w13[0], ffn_w2[0])
    h = _mlstm_layer(h, b, s, norm_mix[1], ml_w_in[0], ml_gate_bias[0], ml_h_gain[0], ml_w_out[0])
    h = _moe_layer(h, norm_ffn[1], moe_router[0], moe_w13[0], moe_w2[0])
    return h.reshape(b, s, d)
```

```python
import functools

import jax
import jax.numpy as jnp
from jax import lax
from jax.experimental import pallas as pl
from jax.experimental.pallas import tpu as pltpu

F32 = jnp.float32
BF16 = jnp.bfloat16

EPS = 1e-6
NEG = -1e30

GRID_W = 64
ROPE_THETA = 10000.0
ATT_HEADS = 32
ATT_KV_HEADS = 8
ATT_HEAD_DIM = 128
ATT_GROUP = ATT_HEADS // ATT_KV_HEADS

ML_HEADS = 8
ML_QK_DIM = 256
ML_V_DIM = 512
GATE_CAP = 15.0
ML_CHUNK = 256

N_EXPERTS = 8
MOE_BM = 512
ROW_DMA_UNROLL = 8

VMEM_LIMIT_BYTES = 56 * 1024 * 1024


def _params(*sem):
    return pltpu.CompilerParams(dimension_semantics=sem, vmem_limit_bytes=VMEM_LIMIT_BYTES)


def _rmsnorm_body(x_ref, g_ref, o_ref):
    x = x_ref[...]
    inv = lax.rsqrt(jnp.mean(x * x, axis=-1, keepdims=True) + EPS)
    o_ref[...] = (x * inv * g_ref[...]).astype(o_ref.dtype)


def rmsnorm(x, gain, out_dtype, bm=256):
    t, d = x.shape
    return pl.pallas_call(
        _rmsnorm_body,
        grid=(t // bm,),
        in_specs=[pl.BlockSpec((bm, d), lambda i: (i, 0)), pl.BlockSpec((1, d), lambda i: (0, 0))],
        out_specs=pl.BlockSpec((bm, d), lambda i: (i, 0)),
        out_shape=jax.ShapeDtypeStruct((t, d), out_dtype),
        compiler_params=_params("parallel"),
        name="rmsnorm",
    )(x, gain.reshape(1, d))


def _mm_body(a_ref, w_ref, o_ref, wb_ref):
    @pl.when(pl.program_id(1) == 0)
    def _():
        wb_ref[...] = w_ref[...].astype(BF16)

    o_ref[...] = jnp.dot(a_ref[...], wb_ref[...], preferred_element_type=F32).astype(o_ref.dtype)


def _mm_res_body(a_ref, w_ref, r_ref, o_ref, wb_ref):
    @pl.when(pl.program_id(1) == 0)
    def _():
        wb_ref[...] = w_ref[...].astype(BF16)

    o_ref[...] = r_ref[...] + jnp.dot(a_ref[...], wb_ref[...], preferred_element_type=F32)


def _mm_wt_body(a_ref, wt_ref, o_ref, wb_ref):
    @pl.when(pl.program_id(1) == 0)
    def _():
        wb_ref[...] = wt_ref[...].T.astype(BF16)

    o_ref[...] = jnp.dot(a_ref[...], wb_ref[...], preferred_element_type=F32).astype(o_ref.dtype)


def matmul(a, w, n, out_dtype, bm, bn, residual=None, w_transposed=False):
    m, k = a.shape
    in_specs = [pl.BlockSpec((bm, k), lambda j, i: (i, 0)), pl.BlockSpec((k, bn), lambda j, i: (0, j))]
    args = [a, w]
    body = _mm_body
    if w_transposed:
        assert residual is None
        in_specs[1] = pl.BlockSpec((bn, k), lambda j, i: (j, 0))
        body = _mm_wt_body
    if residual is not None:
        in_specs.append(pl.BlockSpec((bm, bn), lambda j, i: (i, j)))
        args.append(residual)
        body = _mm_res_body
    return pl.pallas_call(
        body,
        grid=(n // bn, m // bm),
        in_specs=in_specs,
        out_specs=pl.BlockSpec((bm, bn), lambda j, i: (i, j)),
        out_shape=jax.ShapeDtypeStruct((m, n), out_dtype),
        scratch_shapes=[pltpu.VMEM((k, bn), BF16)],
        compiler_params=_params("parallel", "arbitrary"),
        name="matmul",
    )(*args)


def _mm_kgrid_res_body(a_ref, w_ref, r_ref, o_ref):
    part = jnp.dot(a_ref[...], w_ref[...].astype(BF16), preferred_element_type=F32)

    @pl.when(pl.program_id(2) == 0)
    def _():
        o_ref[...] = r_ref[...] + part

    @pl.when(pl.program_id(2) != 0)
    def _():
        o_ref[...] += part


def matmul_kgrid_res(a, w, residual, bm, bn, bk):
    m, k = a.shape
    n = w.shape[1]
    return pl.pallas_call(
        _mm_kgrid_res_body,
        grid=(n // bn, m // bm, k // bk),
        in_specs=[
            pl.BlockSpec((bm, bk), lambda j, i, kk: (i, kk)),
            pl.BlockSpec((bk, bn), lambda j, i, kk: (kk, j)),
            pl.BlockSpec((bm, bn), lambda j, i, kk: (i, j)),
        ],
        out_specs=pl.BlockSpec((bm, bn), lambda j, i, kk: (i, j)),
        out_shape=jax.ShapeDtypeStruct((m, n), F32),
        compiler_params=_params("parallel", "parallel", "arbitrary"),
        name="matmul_kgrid_res",
    )(a, w, residual)


def _silu_mul(g, u):
    return g / (1.0 + jnp.exp(-g)) * u


def _mm_swiglu_body(a_ref, wg_ref, wu_ref, o_ref, wgb_ref, wub_ref):
    @pl.when(pl.program_id(1) == 0)
    def _():
        wgb_ref[...] = wg_ref[...].astype(BF16)
        wub_ref[...] = wu_ref[...].astype(BF16)

    a = a_ref[...]
    g = jnp.dot(a, wgb_ref[...], preferred_element_type=F32)
    u = jnp.dot(a, wub_ref[...], preferred_element_type=F32)
    o_ref[...] = _silu_mul(g, u).astype(o_ref.dtype)


def matmul_swiglu(a, w13, bm, bn):
    m, k = a.shape
    f = w13.shape[1] // 2
    nb = f // bn
    return pl.pallas_call(
        _mm_swiglu_body,
        grid=(nb, m // bm),
        in_specs=[
            pl.BlockSpec((bm, k), lambda j, i: (i, 0)),
            pl.BlockSpec((k, bn), lambda j, i: (0, j)),
            pl.BlockSpec((k, bn), lambda j, i: (0, j + nb)),
        ],
        out_specs=pl.BlockSpec((bm, bn), lambda j, i: (i, j)),
        out_shape=jax.ShapeDtypeStruct((m, f), BF16),
        scratch_shapes=[pltpu.VMEM((k, bn), BF16), pltpu.VMEM((k, bn), BF16)],
        compiler_params=_params("parallel", "arbitrary"),
        name="matmul_swiglu",
    )(a, w13, w13)


def _split_bf16(x):
    hi = x.astype(BF16)
    lo = (x - hi.astype(F32)).astype(BF16)
    return hi, lo


def _norm_mm_hp_body(x_ref, g_ref, w_ref, o_ref, *, w_transposed):
    x = x_ref[...]
    inv = lax.rsqrt(jnp.mean(x * x, axis=-1, keepdims=True) + EPS)
    xn = x * inv * g_ref[...]
    xh, xl = _split_bf16(xn)
    wh, wl = _split_bf16(w_ref[...])
    contract = (((1,), (1 if w_transposed else 0,)), ((), ()))
    acc = lax.dot_general(xh, wh, contract, preferred_element_type=F32)
    acc += lax.dot_general(xh, wl, contract, preferred_element_type=F32)
    acc += lax.dot_general(xl, wh, contract, preferred_element_type=F32)
    o_ref[...] = acc


def norm_matmul_hp(x, gain, w, bm=512, wt_rows=None):
    t, d = x.shape
    if wt_rows is None:
        n = w.shape[1]
        w_spec = pl.BlockSpec((d, n), lambda i: (0, 0))
    else:
        r0, n = wt_rows
        assert r0 % n == 0
        w_spec = pl.BlockSpec((n, d), lambda i: (r0 // n, 0))
    return pl.pallas_call(
        functools.partial(_norm_mm_hp_body, w_transposed=wt_rows is not None),
        grid=(t // bm,),
        in_specs=[
            pl.BlockSpec((bm, d), lambda i: (i, 0)),
            pl.BlockSpec((1, d), lambda i: (0, 0)),
            w_spec,
        ],
        out_specs=pl.BlockSpec((bm, n), lambda i: (i, 0)),
        out_shape=jax.ShapeDtypeStruct((t, n), F32),
        compiler_params=_params("parallel"),
        name="norm_matmul_hp",
    )(x, gain.reshape(1, d), w)


def _qk_rope_body(p_ref, cos_ref, sin_ref, qg_ref, kg_ref, q_ref, kt_ref, v_ref, *, n_q, n_kv, dh, q_scale):
    cos = cos_ref[...]
    sin = sin_ref[...]
    pr = lax.broadcasted_iota(jnp.int32, (dh, dh), 0)
    pc = lax.broadcasted_iota(jnp.int32, (dh, dh), 1)
    swap = jnp.where(jnp.bitwise_xor(pr, 1) == pc, 1.0, 0.0).astype(BF16)

    def norm_rope(x, gain):
        xn = x * lax.rsqrt(jnp.mean(x * x, axis=-1, keepdims=True) + EPS) * gain
        partner = jnp.dot(xn.astype(BF16), swap, preferred_element_type=F32)
        return xn * cos + partner * sin

    for h in range(n_q):
        x = p_ref[:, h * dh:(h + 1) * dh]
        q_ref[:, h * dh:(h + 1) * dh] = (norm_rope(x, qg_ref[...]) * q_scale).astype(q_ref.dtype)
    for h in range(n_kv):
        x = p_ref[:, (n_q + h) * dh:(n_q + h + 1) * dh]
        kt_ref[h * dh:(h + 1) * dh, :] = norm_rope(x, kg_ref[...]).T.astype(kt_ref.dtype)
    v0 = (n_q + n_kv) * dh
    v_ref[...] = p_ref[:, v0:v0 + n_kv * dh].astype(v_ref.dtype)


def qk_norm_rope(proj, cos_full, sin_signed, q_gain, k_gain, batch, seq, n_q, n_kv, dh, bm=256):
    sb = seq // bm
    body = functools.partial(_qk_rope_body, n_q=n_q, n_kv=n_kv, dh=dh, q_scale=dh ** -0.5)
    return pl.pallas_call(
        body,
        grid=(batch, sb),
        in_specs=[
            pl.BlockSpec((bm, proj.shape[1]), lambda b, i: (b * sb + i, 0)),
            pl.BlockSpec((bm, dh), lambda b, i: (i, 0)),
            pl.BlockSpec((bm, dh), lambda b, i: (i, 0)),
            pl.BlockSpec((1, dh), lambda b, i: (0, 0)),
            pl.BlockSpec((1, dh), lambda b, i: (0, 0)),
        ],
        out_specs=[
            pl.BlockSpec((None, bm, n_q * dh), lambda b, i: (b, i, 0)),
            pl.BlockSpec((None, n_kv * dh, bm), lambda b, i: (b, 0, i)),
            pl.BlockSpec((None, bm, n_kv * dh), lambda b, i: (b, i, 0)),
        ],
        out_shape=[
            jax.ShapeDtypeStruct((batch, seq, n_q * dh), BF16),
            jax.ShapeDtypeStruct((batch, n_kv * dh, seq), BF16),
            jax.ShapeDtypeStruct((batch, seq, n_kv * dh), BF16),
        ],
        compiler_params=_params("parallel", "parallel"),
        name="qk_norm_rope",
    )(proj, cos_full, sin_signed, q_gain.reshape(1, dh), k_gain.reshape(1, dh))


def _attn_body(q_ref, kt_ref, v_ref, o_ref, vx_ref, *, group, dh, kb):
    @pl.when(pl.program_id(2) == 0)
    def _():
        vx_ref[:, 0:dh] = v_ref[...]
        vx_ref[:, dh:2 * dh] = jnp.ones((v_ref.shape[0], dh), BF16)

    n_kb = kt_ref.shape[1] // kb
    for g in range(group):
        q = q_ref[:, g * dh:(g + 1) * dh]
        m = None
        acc = None
        for j in range(n_kb):
            s = jnp.dot(q, kt_ref[:, j * kb:(j + 1) * kb], preferred_element_type=F32)
            bmax = jnp.max(s, axis=-1, keepdims=True)
            m_new = bmax if j == 0 else jnp.maximum(m, bmax)
            p = jnp.exp(s - m_new).astype(BF16)
            pv = jnp.dot(p, vx_ref[j * kb:(j + 1) * kb, :], preferred_element_type=F32)
            acc = pv if j == 0 else acc * jnp.exp(m - m_new) + pv
            m = m_new
        o = acc[:, 0:dh] * (1.0 / acc[:, dh:dh + 1])
        o_ref[:, g * dh:(g + 1) * dh] = o.astype(o_ref.dtype)


def attention(q, kt, v, n_kv, group, dh, tq=256, kb=256):
    b, s, _ = q.shape
    body = functools.partial(_attn_body, group=group, dh=dh, kb=min(kb, s))
    return pl.pallas_call(
        body,
        grid=(b, n_kv, s // tq),
        in_specs=[
            pl.BlockSpec((None, tq, group * dh), lambda bi, kv, i: (bi, i, kv)),
            pl.BlockSpec((None, dh, s), lambda bi, kv, i: (bi, kv, 0)),
            pl.BlockSpec((None, s, dh), lambda bi, kv, i: (bi, 0, kv)),
        ],
        out_specs=pl.BlockSpec((None, tq, group * dh), lambda bi, kv, i: (bi, i, kv)),
        out_shape=jax.ShapeDtypeStruct(q.shape, BF16),
        scratch_shapes=[pltpu.VMEM((s, 2 * dh), BF16)],
        compiler_params=_params("parallel", "parallel", "arbitrary"),
        name="attention",
    )(q, kt, v)


def _cumsum_dot(x, tri):
    hi = x.astype(BF16)
    r1 = x - hi.astype(F32)
    mid = r1.astype(BF16)
    lo = (r1 - mid.astype(F32)).astype(BF16)
    acc = jnp.dot(hi, tri, preferred_element_type=F32)
    acc += jnp.dot(mid, tri, preferred_element_type=F32)
    acc += jnp.dot(lo, tri, preferred_element_type=F32)
    return acc


def _gates_body(pre_ref, bias_ref, o_ref, *, heads, chunk_shift):
    g = pre_ref[...] + bias_ref[...]
    g = GATE_CAP * jnp.tanh(g / GATE_CAP)
    logsig = jnp.minimum(g, 0.0) - jnp.log1p(jnp.exp(-jnp.abs(g)))
    n = g.shape[1]
    t = lax.broadcasted_iota(jnp.int32, (n, n), 0)
    j = lax.broadcasted_iota(jnp.int32, (n, n), 1)
    same = lax.shift_right_logical(t, chunk_shift) == lax.shift_right_logical(j, chunk_shift)
    tri_f = jnp.where(jnp.logical_and(same, t <= j), 1.0, 0.0).astype(BF16)
    tri_b = jnp.where(jnp.logical_and(same, t >= j), 1.0, 0.0).astype(BF16)
    h = heads
    bc_f = _cumsum_dot(logsig[h:2 * h], tri_f)
    suf_b = _cumsum_dot(logsig[3 * h:4 * h], tri_b)
    o_ref[0:h, :] = bc_f
    o_ref[h:2 * h, :] = g[0:h] - bc_f
    o_ref[2 * h:3 * h, :] = suf_b
    o_ref[3 * h:4 * h, :] = g[2 * h:3 * h] - suf_b


def mlstm_gates(pre_rows, bias, heads, chunk, sb=512):
    b, r, s = pre_rows.shape
    body = functools.partial(_gates_body, heads=heads, chunk_shift=chunk.bit_length() - 1)
    return pl.pallas_call(
        body,
        grid=(b, s // sb),
        in_specs=[
            pl.BlockSpec((None, r, sb), lambda bi, i: (bi, 0, i)),
            pl.BlockSpec((r, 1), lambda bi, i: (0, 0)),
        ],
        out_specs=pl.BlockSpec((None, r, sb), lambda bi, i: (bi, 0, i)),
        out_shape=jax.ShapeDtypeStruct((b, r, s), F32),
        compiler_params=_params("parallel", "parallel"),
        name="mlstm_gates",
    )(pre_rows, bias.reshape(r, 1))


def _mlstm_chunk(q_ref, k_ref, v_ref, gcol_ref, grow_ref, h_ref, c_ref, n_ref, m_ref, hd, *, heads, q_scale, backward):
    q = q_ref[...]
    k = k_ref[...]
    v = v_ref[...]
    ln = q.shape[0]
    gcol = gcol_ref[...]
    lane = lax.broadcasted_iota(jnp.int32, gcol.shape, 1)
    base = (2 * heads if backward else 0) + hd
    bcol = jnp.sum(jnp.where(lane == base, gcol, 0.0), axis=1, keepdims=True)
    rcol = jnp.sum(jnp.where(lane == base + heads, gcol, 0.0), axis=1, keepdims=True)
    rrow = grow_ref[pl.ds(base + heads, 1), :]

    row_i = lax.broadcasted_iota(jnp.int32, (ln, ln), 0)
    col_i = lax.broadcasted_iota(jnp.int32, (ln, ln), 1)
    mask = (col_i >= row_i) if backward else (col_i <= row_i)
    dmat = jnp.where(mask, bcol + rrow, NEG)
    m_prev = m_ref[...]
    inter = bcol + m_prev
    m_row = jnp.maximum(jnp.max(dmat, axis=1, keepdims=True), inter)
    qk = lax.dot_general(q, k, (((1,), (1,)), ((), ())), preferred_element_type=F32) * q_scale
    s = qk * jnp.exp(dmat - m_row)
    decay = jnp.exp(inter - m_row)
    q_c = jnp.dot(q, c_ref[...].astype(BF16), preferred_element_type=F32) * q_scale
    num = jnp.dot(s.astype(BF16), v, preferred_element_type=F32) + decay * q_c
    q_n = jnp.sum(q.astype(F32) * n_ref[...], axis=1, keepdims=True) * q_scale
    den = jnp.sum(s, axis=1, keepdims=True) + decay * q_n
    h_ref[...] = num / jnp.maximum(jnp.abs(den), jnp.exp(-m_row))

    g_last = bcol[0:1, :] if backward else bcol[ln - 1:ln, :]
    wcol = g_last + rcol
    m_new = jnp.maximum(g_last + m_prev, jnp.max(wcol, axis=0, keepdims=True))
    carry_decay = jnp.exp(g_last + m_prev - m_new)
    wk = jnp.exp(wcol - m_new) * k.astype(F32)
    kv = lax.dot_general(wk.astype(BF16), v, (((0,), (0,)), ((), ())), preferred_element_type=F32)
    c_ref[...] = carry_decay * c_ref[...] + kv
    n_ref[...] = carry_decay * n_ref[...] + jnp.sum(wk, axis=0, keepdims=True)
    m_ref[...] = m_new


def _mlstm_body(qf_ref, kf_ref, vf_ref, gcf_ref, grf_ref, qb_ref, kb_ref, vb_ref, gcb_ref, grb_ref,
                hf_ref, hb_ref, c_ref, n_ref, m_ref, *, heads, q_scale):
    hd = pl.program_id(1)

    @pl.when(pl.program_id(2) == 0)
    def _():
        c_ref[...] = jnp.zeros(c_ref.shape, F32)
        n_ref[...] = jnp.zeros(n_ref.shape, F32)
        m_ref[...] = jnp.full(m_ref.shape, NEG, F32)

    chunk = functools.partial(_mlstm_chunk, heads=heads, q_scale=q_scale)
    chunk(qf_ref, kf_ref, vf_ref, gcf_ref, grf_ref, hf_ref, c_ref.at[0], n_ref.at[0], m_ref.at[0], hd,
          backward=False)
    chunk(qb_ref, kb_ref, vb_ref, gcb_ref, grb_ref, hb_ref, c_ref.at[1], n_ref.at[1], m_ref.at[1], hd,
          backward=True)


def mlstm_scan(proj, gcol, grow, heads, dk, dv, chunk):
    b, s, _ = proj.shape
    nc = s // chunk
    body = functools.partial(_mlstm_body, heads=heads, q_scale=dk ** -0.5)
    k_blk0 = heads
    v_blk0 = (2 * heads * dk) // dv

    def direction_specs(cidx):
        return [
            pl.BlockSpec((None, chunk, dk), lambda bi, h, c: (bi, cidx(c), h)),
            pl.BlockSpec((None, chunk, dk), lambda bi, h, c: (bi, cidx(c), k_blk0 + h)),
            pl.BlockSpec((None, chunk, dv), lambda bi, h, c: (bi, cidx(c), v_blk0 + h)),
            pl.BlockSpec((None, chunk, 4 * heads), lambda bi, h, c: (bi, cidx(c), 0)),
            pl.BlockSpec((None, 4 * heads, chunk), lambda bi, h, c: (bi, 0, cidx(c))),
        ]

    def fwd(c):
        return c

    def bwd(c):
        return nc - 1 - c

    out_sds = jax.ShapeDtypeStruct((b, s, heads * dv), F32)
    return pl.pallas_call(
        body,
        grid=(b, heads, nc),
        in_specs=direction_specs(fwd) + direction_specs(bwd),
        out_specs=[
            pl.BlockSpec((None, chunk, dv), lambda bi, h, c: (bi, fwd(c), h)),
            pl.BlockSpec((None, chunk, dv), lambda bi, h, c: (bi, bwd(c), h)),
        ],
        out_shape=[out_sds, out_sds],
        scratch_shapes=[
            pltpu.VMEM((2, dk, dv), F32),
            pltpu.VMEM((2, 1, dk), F32),
            pltpu.VMEM((2, 1, 1), F32),
        ],
        compiler_params=_params("parallel", "parallel", "arbitrary"),
        name="mlstm_scan",
    )(*([proj, proj, proj, gcol, grow] * 2))


def _ml_out_body(hf_ref, hb_ref, o_ref, g_ref, out_ref, *, heads, dv):
    for h in range(heads):
        sl = slice(h * dv, (h + 1) * dv)
        x = hf_ref[:, sl] + hb_ref[:, sl]
        xn = x * lax.rsqrt(jnp.mean(x * x, axis=-1, keepdims=True) + EPS) * g_ref[:, sl]
        og = o_ref[:, sl].astype(F32)
        out_ref[:, sl] = (xn / (1.0 + jnp.exp(-og))).astype(out_ref.dtype)


def mlstm_out_gate(h_fwd, h_bwd, proj, h_gain, heads, dv, o_blk, bm=256):
    t, d = h_fwd.shape
    body = functools.partial(_ml_out_body, heads=heads, dv=dv)
    return pl.pallas_call(
        body,
        grid=(t // bm,),
        in_specs=[
            pl.BlockSpec((bm, d), lambda i: (i, 0)),
            pl.BlockSpec((bm, d), lambda i: (i, 0)),
            pl.BlockSpec((bm, d), lambda i: (i, o_blk)),
            pl.BlockSpec((1, d), lambda i: (0, 0)),
        ],
        out_specs=pl.BlockSpec((bm, d), lambda i: (i, 0)),
        out_shape=jax.ShapeDtypeStruct((t, d), BF16),
        compiler_params=_params("parallel"),
        name="mlstm_out_gate",
    )(h_fwd, h_bwd, proj, h_gain.reshape(1, d))


def _route_body(lg_ref, idx_ref, rank_ref, w_ref, cnt_ref, carry_ref):
    @pl.when(pl.program_id(0) == 0)
    def _():
        carry_ref[...] = jnp.zeros(carry_ref.shape, F32)

    lg = lg_ref[...]
    n_e, tb = lg.shape
    e_iota = lax.broadcasted_iota(jnp.int32, lg.shape, 0).astype(F32)
    t1 = jnp.max(lg, axis=0, keepdims=True)
    i1 = jnp.min(jnp.where(lg == t1, e_iota, float(n_e)), axis=0, keepdims=True)
    first = e_iota == i1
    lg2 = jnp.where(first, -jnp.inf, lg)
    t2 = jnp.max(lg2, axis=0, keepdims=True)
    i2 = jnp.min(jnp.where(lg2 == t2, e_iota, float(n_e)), axis=0, keepdims=True)
    second = e_iota == i2
    e2 = jnp.exp(t2 - t1)
    w_ref[0:1, :] = 1.0 / (1.0 + e2)
    w_ref[1:2, :] = e2 / (1.0 + e2)
    idx_ref[0:1, :] = i1.astype(jnp.int32)
    idx_ref[1:2, :] = i2.astype(jnp.int32)

    assign = jnp.where(first, 1.0, 0.0) + jnp.where(second, 1.0, 0.0)
    tp = lax.broadcasted_iota(jnp.int32, (tb, tb), 0)
    tc = lax.broadcasted_iota(jnp.int32, (tb, tb), 1)
    before = jnp.where(tp < tc, 1.0, 0.0).astype(BF16)
    rank = jnp.dot(assign.astype(BF16), before, preferred_element_type=F32) + carry_ref[:, 0:1]
    rank_ref[0:1, :] = jnp.sum(jnp.where(first, rank, 0.0), axis=0, keepdims=True).astype(jnp.int32)
    rank_ref[1:2, :] = jnp.sum(jnp.where(second, rank, 0.0), axis=0, keepdims=True).astype(jnp.int32)
    carry_ref[...] = carry_ref[...] + jnp.sum(assign, axis=1, keepdims=True)
    cnt_ref[...] = carry_ref[...]


def route_top2(logits_rows, tb=512):
    n_e, t = logits_rows.shape
    return pl.pallas_call(
        _route_body,
        grid=(t // tb,),
        in_specs=[pl.BlockSpec((n_e, tb), lambda i: (0, i))],
        out_specs=[
            pl.BlockSpec((2, tb), lambda i: (0, i)),
            pl.BlockSpec((2, tb), lambda i: (0, i)),
            pl.BlockSpec((2, tb), lambda i: (0, i)),
            pl.BlockSpec((n_e, 128), lambda i: (0, 0)),
        ],
        out_shape=[
            jax.ShapeDtypeStruct((2, t), jnp.int32),
            jax.ShapeDtypeStruct((2, t), jnp.int32),
            jax.ShapeDtypeStruct((2, t), F32),
            jax.ShapeDtypeStruct((n_e, 128), F32),
        ],
        scratch_shapes=[pltpu.VMEM((n_e, 128), F32)],
        compiler_params=_params("arbitrary"),
        name="route_top2",
    )(logits_rows)


def _gather_norm_body(src_ref, used_ref, x_hbm, g_ref, o_ref, xbuf, sem, *, tb):
    i = pl.program_id(0)
    n_used = used_ref[0]

    def row_copy(src_row, slot, j):
        return pltpu.make_async_copy(x_hbm.at[pl.ds(src_row, 1)], xbuf.at[slot, pl.ds(j, 1)], sem.at[slot])

    def issue_tile(tile):
        slot = tile % 2

        def issue(j, carry):
            row_copy(src_ref[tile * tb + j], slot, j).start()
            return carry

        lax.fori_loop(0, tb, issue, 0, unroll=ROW_DMA_UNROLL)

    @pl.when(i == 0)
    def _():
        issue_tile(i)

    @pl.when(i + 1 < n_used)
    def _():
        issue_tile(i + 1)

    @pl.when(i < n_used)
    def _():
        slot = i % 2
        pltpu.make_async_copy(x_hbm.at[pl.ds(0, tb)], xbuf.at[slot], sem.at[slot]).wait()
        x = xbuf[slot]
        inv = lax.rsqrt(jnp.mean(x * x, axis=-1, keepdims=True) + EPS)
        o_ref[...] = (x * inv * g_ref[...]).astype(o_ref.dtype)

    @pl.when(i >= n_used)
    def _():
        o_ref[...] = jnp.zeros(o_ref.shape, o_ref.dtype)


def moe_gather_norm(x, gain, src_rows, n_used, tb):
    t, d = x.shape
    r = src_rows.shape[0]
    body = functools.partial(_gather_norm_body, tb=tb)
    return pl.pallas_call(
        body,
        grid_spec=pltpu.PrefetchScalarGridSpec(
            num_scalar_prefetch=2,
            grid=(r // tb,),
            in_specs=[pl.BlockSpec(memory_space=pl.ANY), pl.BlockSpec((1, d), lambda i, src, nu: (0, 0))],
            out_specs=pl.BlockSpec((tb, d), lambda i, src, nu: (i, 0)),
            scratch_shapes=[pltpu.VMEM((2, tb, d), F32), pltpu.SemaphoreType.DMA((2,))],
        ),
        out_shape=jax.ShapeDtypeStruct((r, d), BF16),
        compiler_params=_params("arbitrary"),
        name="moe_gather_norm",
    )(src_rows, n_used, x, gain.reshape(1, d))


def _new_weight_block(exp_ref):
    i = pl.program_id(1)
    return jnp.logical_or(i == 0, exp_ref[i] != exp_ref[jnp.maximum(i - 1, 0)])


def _gmm_swiglu_body(src_ref, exp_ref, used_ref, x_ref, wg_ref, wu_ref, o_ref, wgb_ref, wub_ref):
    del src_ref

    @pl.when(_new_weight_block(exp_ref))
    def _():
        wgb_ref[...] = wg_ref[...].astype(BF16)
        wub_ref[...] = wu_ref[...].astype(BF16)

    @pl.when(pl.program_id(1) < used_ref[0])
    def _():
        a = x_ref[...]
        g = jnp.dot(a, wgb_ref[...], preferred_element_type=F32)
        u = jnp.dot(a, wub_ref[...], preferred_element_type=F32)
        o_ref[...] = _silu_mul(g, u).astype(o_ref.dtype)

    @pl.when(pl.program_id(1) >= used_ref[0])
    def _():
        o_ref[...] = jnp.zeros(o_ref.shape, o_ref.dtype)


def grouped_swiglu(xs, w13, tile_src, tile_exp, n_used, bm, bn):
    r, d = xs.shape
    f = w13.shape[2] // 2
    nb = f // bn
    return pl.pallas_call(
        _gmm_swiglu_body,
        grid_spec=pltpu.PrefetchScalarGridSpec(
            num_scalar_prefetch=3,
            grid=(nb, r // bm),
            in_specs=[
                pl.BlockSpec((bm, d), lambda j, i, src, ex, nu: (src[i], 0)),
                pl.BlockSpec((None, d, bn), lambda j, i, src, ex, nu: (ex[i], 0, j)),
                pl.BlockSpec((None, d, bn), lambda j, i, src, ex, nu: (ex[i], 0, j + nb)),
            ],
            out_specs=pl.BlockSpec((bm, bn), lambda j, i, src, ex, nu: (i, j)),
            scratch_shapes=[pltpu.VMEM((d, bn), BF16), pltpu.VMEM((d, bn), BF16)],
        ),
        out_shape=jax.ShapeDtypeStruct((r, f), BF16),
        compiler_params=_params("arbitrary", "arbitrary"),
        name="grouped_swiglu",
    )(tile_src, tile_exp, n_used, xs, w13, w13)


def _gmm_body(src_ref, exp_ref, used_ref, a_ref, w_ref, o_ref, wb_ref):
    del src_ref

    @pl.when(_new_weight_block(exp_ref))
    def _():
        wb_ref[...] = w_ref[...].astype(BF16)

    @pl.when(pl.program_id(1) < used_ref[0])
    def _():
        o_ref[...] = jnp.dot(a_ref[...], wb_ref[...], preferred_element_type=F32).astype(o_ref.dtype)

    @pl.when(pl.program_id(1) >= used_ref[0])
    def _():
        o_ref[...] = jnp.zeros(o_ref.shape, o_ref.dtype)


def grouped_matmul(a, w, tile_src, tile_exp, n_used, bm, bn):
    r, k = a.shape
    n = w.shape[2]
    return pl.pallas_call(
        _gmm_body,
        grid_spec=pltpu.PrefetchScalarGridSpec(
            num_scalar_prefetch=3,
            grid=(n // bn, r // bm),
            in_specs=[
                pl.BlockSpec((bm, k), lambda j, i, src, ex, nu: (src[i], 0)),
                pl.BlockSpec((None, k, bn), lambda j, i, src, ex, nu: (ex[i], 0, j)),
            ],
            out_specs=pl.BlockSpec((bm, bn), lambda j, i, src, ex, nu: (i, j)),
            scratch_shapes=[pltpu.VMEM((k, bn), BF16)],
        ),
        out_shape=jax.ShapeDtypeStruct((r, n), F32),
        compiler_params=_params("arbitrary", "arbitrary"),
        name="grouped_matmul",
    )(tile_src, tile_exp, n_used, a, w)


def _combine_body(pos_ref, x_ref, w_ref, y_hbm, o_ref, ybuf, sem, *, tb, t_total):
    i = pl.program_id(0)

    def row_copy(src_row, slot, choice, j):
        return pltpu.make_async_copy(
            y_hbm.at[pl.ds(src_row, 1)], ybuf.at[slot, choice, pl.ds(j, 1)], sem.at[slot])

    def issue_tile(tile):
        slot = tile % 2

        def issue(j, carry):
            t = tile * tb + j
            row_copy(pos_ref[t], slot, 0, j).start()
            row_copy(pos_ref[t_total + t], slot, 1, j).start()
            return carry

        lax.fori_loop(0, tb, issue, 0, unroll=ROW_DMA_UNROLL)

    @pl.when(i == 0)
    def _():
        issue_tile(i)

    @pl.when(i + 1 < pl.num_programs(0))
    def _():
        issue_tile(i + 1)

    slot = i % 2
    for choice in range(2):
        pltpu.make_async_copy(y_hbm.at[pl.ds(0, tb)], ybuf.at[slot, choice], sem.at[slot]).wait()
    w = w_ref[...]
    o_ref[...] = x_ref[...] + w[:, 0:1] * ybuf[slot, 0] + w[:, 1:2] * ybuf[slot, 1]


def moe_combine(x, y, pos_flat, w_cols, tb=256):
    t, d = x.shape
    body = functools.partial(_combine_body, tb=tb, t_total=t)
    return pl.pallas_call(
        body,
        grid_spec=pltpu.PrefetchScalarGridSpec(
            num_scalar_prefetch=1,
            grid=(t // tb,),
            in_specs=[
                pl.BlockSpec((tb, d), lambda i, pos: (i, 0)),
                pl.BlockSpec((tb, 2), lambda i, pos: (i, 0)),
                pl.BlockSpec(memory_space=pl.ANY),
            ],
            out_specs=pl.BlockSpec((tb, d), lambda i, pos: (i, 0)),
            scratch_shapes=[pltpu.VMEM((2, 2, tb, d), F32), pltpu.SemaphoreType.DMA((2,))],
        ),
        out_shape=jax.ShapeDtypeStruct((t, d), F32),
        compiler_params=_params("arbitrary"),
        name="moe_combine",
    )(pos_flat, x, w_cols, y)


def moe_plan(idx, rank, counts, bm, max_tiles):
    t = idx.shape[1]
    tiles = (counts + bm - 1) // bm
    tile_end = jnp.cumsum(tiles)
    row_off = (tile_end - tiles) * bm
    experts = jnp.arange(tiles.shape[0], dtype=idx.dtype)[:, None, None]
    row_base = jnp.sum(jnp.where(idx[None] == experts, row_off[:, None, None], 0), axis=0)
    pos = (row_base + rank).reshape(-1).astype(jnp.int32)
    tokens = jnp.tile(jnp.arange(t, dtype=jnp.int32), 2)
    src_rows = jnp.zeros((max_tiles * bm,), jnp.int32).at[pos].set(tokens)
    n_used = tile_end[-1]
    tile_src = jnp.minimum(jnp.arange(max_tiles, dtype=jnp.int32), n_used - 1)
    tile_exp = jnp.sum(tile_src[:, None] >= tile_end[None, :], axis=1).astype(jnp.int32)
    return pos, src_rows, tile_src, tile_exp, n_used.reshape(1).astype(jnp.int32)


def _rope_tables(seq):
    rows = seq // GRID_W
    row_ids = jnp.repeat(jnp.arange(rows, dtype=F32), GRID_W)
    col_ids = jnp.tile(jnp.arange(GRID_W, dtype=F32), rows)
    n_freq = ATT_HEAD_DIM // 4
    inv_freq = ROPE_THETA ** (-jnp.arange(n_freq, dtype=F32) / n_freq)
    ang = jnp.concatenate([row_ids[:, None] * inv_freq, col_ids[:, None] * inv_freq], axis=-1)
    cos, sin = jnp.cos(ang), jnp.sin(ang)
    cos_full = jnp.repeat(cos, 2, axis=-1)
    sin_signed = jnp.stack([-sin, sin], axis=-1).reshape(seq, ATT_HEAD_DIM)
    return cos_full, sin_signed


def _attention_layer(x, b, s, norm_g, w_in, q_gain, k_gain, w_out):
    t, d = x.shape
    hn = rmsnorm(x, norm_g, BF16)
    proj = matmul(hn, w_in, w_in.shape[1], F32, bm=1024, bn=512)
    cos_full, sin_signed = _rope_tables(s)
    q, kt, v = qk_norm_rope(proj, cos_full, sin_signed, q_gain, k_gain, b, s,
                            ATT_HEADS, ATT_KV_HEADS, ATT_HEAD_DIM)
    o = attention(q, kt, v, ATT_KV_HEADS, ATT_GROUP, ATT_HEAD_DIM)
    return matmul(o.reshape(t, -1), w_out, d, F32, bm=1024, bn=512, residual=x)


def _dense_ffn_layer(x, norm_g, w13, w2):
    hn = rmsnorm(x, norm_g, BF16)
    act = matmul_swiglu(hn, w13, bm=512, bn=512)
    return matmul_kgrid_res(act, w2, x, bm=1024, bn=1024, bk=2048)


def _mlstm_layer(x, b, s, norm_g, w_in, gate_bias, h_gain, w_out):
    t, d = x.shape
    n_main = 2 * ML_HEADS * ML_QK_DIM + 2 * ML_HEADS * ML_V_DIM
    hn = rmsnorm(x, norm_g, BF16)
    w_in_t = w_in.T
    proj = matmul(hn, w_in_t, n_main, BF16, bm=1024, bn=512, w_transposed=True)
    pre = norm_matmul_hp(x, norm_g, w_in_t, wt_rows=(n_main, 4 * ML_HEADS))
    pre_rows = pre.reshape(b, s, 4 * ML_HEADS).transpose(0, 2, 1)
    grow = mlstm_gates(pre_rows, gate_bias, ML_HEADS, ML_CHUNK)
    gcol = grow.transpose(0, 2, 1)
    h_fwd, h_bwd = mlstm_scan(proj.reshape(b, s, n_main), gcol, grow, ML_HEADS, ML_QK_DIM, ML_V_DIM, ML_CHUNK)
    o_blk = (2 * ML_HEADS * ML_QK_DIM + ML_HEADS * ML_V_DIM) // (ML_HEADS * ML_V_DIM)
    gated = mlstm_out_gate(h_fwd.reshape(t, -1), h_bwd.reshape(t, -1), proj, h_gain, ML_HEADS, ML_V_DIM, o_blk)
    return matmul(gated, w_out, d, F32, bm=1024, bn=512, residual=x)


def _moe_layer(x, norm_g, router, w13, w2):
    t, d = x.shape
    logits = norm_matmul_hp(x, norm_g, router)
    idx, rank, w_rows, cnt = route_top2(logits.T)
    max_tiles = (2 * t) // MOE_BM + N_EXPERTS
    pos, src_rows, tile_src, tile_exp, n_used = moe_plan(idx, rank, cnt[:, 0].astype(jnp.int32), MOE_BM, max_tiles)
    xs = moe_gather_norm(x, norm_g, src_rows, n_used, MOE_BM)
    act = grouped_swiglu(xs, w13, tile_src, tile_exp, n_used, MOE_BM, 512)
    y = grouped_matmul(act, w2, tile_src, tile_exp, n_used, MOE_BM, 512)
    return moe_combine(x, y, pos, w_rows.T)


def kernel(x, norm_mix, norm_ffn, att_w_in, att_q_gain, att_k_gain, att_w_out, ffn_w13, ffn_w2,
           ml_w_in, ml_gate_bias, ml_h_gain, ml_w_out, moe_router, moe_w13, moe_w2):
    b, s, d = x.shape
    h = x.reshape(b * s, d)
    h = _attention_layer(h, b, s, norm_mix[0], att_w_in[0], att_q_gain[0], att_k_gain[0], att_w_out[0])
    h = _dense_ffn_layer(h, norm_ffn[0], ffn_w13[0], ffn_w2[0])
    h = _mlstm_layer(h, b, s, norm_mix[1], ml_w_in[0], ml_gate_bias[0], ml_h_gain[0], ml_w_out[0])
    h = _moe_layer(h, norm_ffn[1], moe_router[0], moe_w13[0], moe_w2[0])
    return h.reshape(b, s, d)
```

```python
import functools

import jax
import jax.numpy as jnp
from jax import lax
from jax.experimental import pallas as pl
from jax.experimental.pallas import tpu as pltpu

F32 = jnp.float32
BF16 = jnp.bfloat16

EPS = 1e-6
NEG = -1e30
LOG2_E = 1.4426950408889634

GRID_W = 64
ROPE_THETA = 10000.0
ATT_HEADS = 32
ATT_KV_HEADS = 8
ATT_HEAD_DIM = 128
ATT_GROUP = ATT_HEADS // ATT_KV_HEADS

ML_HEADS = 8
ML_QK_DIM = 256
ML_V_DIM = 512
GATE_CAP = 15.0
ML_CHUNK = 256

N_EXPERTS = 8
MOE_BM = 512
ROW_DMA_UNROLL = 8

VMEM_LIMIT_BYTES = 56 * 1024 * 1024


def _params(*sem):
    return pltpu.CompilerParams(dimension_semantics=sem, vmem_limit_bytes=VMEM_LIMIT_BYTES)


def _rmsnorm_body(x_ref, g_ref, o_ref):
    x = x_ref[...]
    inv = lax.rsqrt(jnp.mean(x * x, axis=-1, keepdims=True) + EPS)
    o_ref[...] = (x * inv * g_ref[...]).astype(o_ref.dtype)


def rmsnorm(x, gain, out_dtype, bm=256):
    t, d = x.shape
    return pl.pallas_call(
        _rmsnorm_body,
        grid=(t // bm,),
        in_specs=[pl.BlockSpec((bm, d), lambda i: (i, 0)), pl.BlockSpec((1, d), lambda i: (0, 0))],
        out_specs=pl.BlockSpec((bm, d), lambda i: (i, 0)),
        out_shape=jax.ShapeDtypeStruct((t, d), out_dtype),
        compiler_params=_params("parallel"),
        name="rmsnorm",
    )(x, gain.reshape(1, d))


def _mm_body(a_ref, w_ref, o_ref, wb_ref):
    @pl.when(pl.program_id(1) == 0)
    def _():
        wb_ref[...] = w_ref[...].astype(BF16)

    o_ref[...] = jnp.dot(a_ref[...], wb_ref[...], preferred_element_type=F32).astype(o_ref.dtype)


def _mm_res_body(a_ref, w_ref, r_ref, o_ref, wb_ref):
    @pl.when(pl.program_id(1) == 0)
    def _():
        wb_ref[...] = w_ref[...].astype(BF16)

    o_ref[...] = r_ref[...] + jnp.dot(a_ref[...], wb_ref[...], preferred_element_type=F32)


def _mm_wt_body(a_ref, wt_ref, o_ref, wb_ref):
    @pl.when(pl.program_id(1) == 0)
    def _():
        wb_ref[...] = wt_ref[...].T.astype(BF16)

    o_ref[...] = jnp.dot(a_ref[...], wb_ref[...], preferred_element_type=F32).astype(o_ref.dtype)


def matmul(a, w, n, out_dtype, bm, bn, residual=None, w_transposed=False):
    m, k = a.shape
    in_specs = [pl.BlockSpec((bm, k), lambda j, i: (i, 0)), pl.BlockSpec((k, bn), lambda j, i: (0, j))]
    args = [a, w]
    body = _mm_body
    if w_transposed:
        assert residual is None
        in_specs[1] = pl.BlockSpec((bn, k), lambda j, i: (j, 0))
        body = _mm_wt_body
    if residual is not None:
        in_specs.append(pl.BlockSpec((bm, bn), lambda j, i: (i, j)))
        args.append(residual)
        body = _mm_res_body
    return pl.pallas_call(
        body,
        grid=(n // bn, m // bm),
        in_specs=in_specs,
        out_specs=pl.BlockSpec((bm, bn), lambda j, i: (i, j)),
        out_shape=jax.ShapeDtypeStruct((m, n), out_dtype),
        scratch_shapes=[pltpu.VMEM((k, bn), BF16)],
        compiler_params=_params("parallel", "arbitrary"),
        name="matmul",
    )(*args)


def _mm_kgrid_res_body(a_ref, w_ref, r_ref, o_ref):
    part = jnp.dot(a_ref[...], w_ref[...].astype(BF16), preferred_element_type=F32)

    @pl.when(pl.program_id(2) == 0)
    def _():
        o_ref[...] = r_ref[...] + part

    @pl.when(pl.program_id(2) != 0)
    def _():
        o_ref[...] += part


def matmul_kgrid_res(a, w, residual, bm, bn, bk):
    m, k = a.shape
    n = w.shape[1]
    return pl.pallas_call(
        _mm_kgrid_res_body,
        grid=(n // bn, m // bm, k // bk),
        in_specs=[
            pl.BlockSpec((bm, bk), lambda j, i, kk: (i, kk)),
            pl.BlockSpec((bk, bn), lambda j, i, kk: (kk, j)),
            pl.BlockSpec((bm, bn), lambda j, i, kk: (i, j)),
        ],
        out_specs=pl.BlockSpec((bm, bn), lambda j, i, kk: (i, j)),
        out_shape=jax.ShapeDtypeStruct((m, n), F32),
        compiler_params=_params("parallel", "parallel", "arbitrary"),
        name="matmul_kgrid_res",
    )(a, w, residual)


def _silu_mul(g, u):
    return g / (1.0 + jnp.exp(-g)) * u


def _mm_swiglu_body(a_ref, wg_ref, wu_ref, o_ref, wgb_ref, wub_ref):
    @pl.when(pl.program_id(1) == 0)
    def _():
        wgb_ref[...] = wg_ref[...].astype(BF16)
        wub_ref[...] = wu_ref[...].astype(BF16)

    a = a_ref[...]
    g = jnp.dot(a, wgb_ref[...], preferred_element_type=F32)
    u = jnp.dot(a, wub_ref[...], preferred_element_type=F32)
    o_ref[...] = _silu_mul(g, u).astype(o_ref.dtype)


def matmul_swiglu(a, w13, bm, bn):
    m, k = a.shape
    f = w13.shape[1] // 2
    nb = f // bn
    return pl.pallas_call(
        _mm_swiglu_body,
        grid=(nb, m // bm),
        in_specs=[
            pl.BlockSpec((bm, k), lambda j, i: (i, 0)),
            pl.BlockSpec((k, bn), lambda j, i: (0, j)),
            pl.BlockSpec((k, bn), lambda j, i: (0, j + nb)),
        ],
        out_specs=pl.BlockSpec((bm, bn), lambda j, i: (i, j)),
        out_shape=jax.ShapeDtypeStruct((m, f), BF16),
        scratch_shapes=[pltpu.VMEM((k, bn), BF16), pltpu.VMEM((k, bn), BF16)],
        compiler_params=_params("parallel", "arbitrary"),
        name="matmul_swiglu",
    )(a, w13, w13)


def _split_bf16(x):
    hi = x.astype(BF16)
    lo = (x - hi.astype(F32)).astype(BF16)
    return hi, lo


def _norm_mm_hp_body(x_ref, g_ref, w_ref, o_ref, *, w_transposed):
    x = x_ref[...]
    inv = lax.rsqrt(jnp.mean(x * x, axis=-1, keepdims=True) + EPS)
    xn = x * inv * g_ref[...]
    xh, xl = _split_bf16(xn)
    wh, wl = _split_bf16(w_ref[...])
    contract = (((1,), (1 if w_transposed else 0,)), ((), ()))
    acc = lax.dot_general(xh, wh, contract, preferred_element_type=F32)
    acc += lax.dot_general(xh, wl, contract, preferred_element_type=F32)
    acc += lax.dot_general(xl, wh, contract, preferred_element_type=F32)
    o_ref[...] = acc


def norm_matmul_hp(x, gain, w, bm=512, wt_rows=None):
    t, d = x.shape
    if wt_rows is None:
        n = w.shape[1]
        w_spec = pl.BlockSpec((d, n), lambda i: (0, 0))
    else:
        r0, n = wt_rows
        assert r0 % n == 0
        w_spec = pl.BlockSpec((n, d), lambda i: (r0 // n, 0))
    return pl.pallas_call(
        functools.partial(_norm_mm_hp_body, w_transposed=wt_rows is not None),
        grid=(t // bm,),
        in_specs=[
            pl.BlockSpec((bm, d), lambda i: (i, 0)),
            pl.BlockSpec((1, d), lambda i: (0, 0)),
            w_spec,
        ],
        out_specs=pl.BlockSpec((bm, n), lambda i: (i, 0)),
        out_shape=jax.ShapeDtypeStruct((t, n), F32),
        compiler_params=_params("parallel"),
        name="norm_matmul_hp",
    )(x, gain.reshape(1, d), w)


def _qk_rope_body(p_ref, cos_ref, sin_ref, qg_ref, kg_ref, q_ref, kt_ref, v_ref, *, n_q, n_kv, dh, q_scale):
    cos = cos_ref[...]
    sin = sin_ref[...]
    pr = lax.broadcasted_iota(jnp.int32, (dh, dh), 0)
    pc = lax.broadcasted_iota(jnp.int32, (dh, dh), 1)
    swap = jnp.where(jnp.bitwise_xor(pr, 1) == pc, 1.0, 0.0).astype(BF16)

    def norm_rope(x, gain):
        xn = x * lax.rsqrt(jnp.mean(x * x, axis=-1, keepdims=True) + EPS) * gain
        partner = jnp.dot(xn.astype(BF16), swap, preferred_element_type=F32)
        return xn * cos + partner * sin

    for h in range(n_q):
        x = p_ref[:, h * dh:(h + 1) * dh]
        q_ref[:, h * dh:(h + 1) * dh] = (norm_rope(x, qg_ref[...]) * q_scale).astype(q_ref.dtype)
    for h in range(n_kv):
        x = p_ref[:, (n_q + h) * dh:(n_q + h + 1) * dh]
        kt_ref[h * dh:(h + 1) * dh, :] = norm_rope(x, kg_ref[...]).T.astype(kt_ref.dtype)
    v0 = (n_q + n_kv) * dh
    v_ref[...] = p_ref[:, v0:v0 + n_kv * dh].astype(v_ref.dtype)


def qk_norm_rope(proj, cos_full, sin_signed, q_gain, k_gain, batch, seq, n_q, n_kv, dh, bm=256):
    sb = seq // bm
    body = functools.partial(_qk_rope_body, n_q=n_q, n_kv=n_kv, dh=dh, q_scale=dh ** -0.5 * LOG2_E)
    return pl.pallas_call(
        body,
        grid=(batch, sb),
        in_specs=[
            pl.BlockSpec((bm, proj.shape[1]), lambda b, i: (b * sb + i, 0)),
            pl.BlockSpec((bm, dh), lambda b, i: (i, 0)),
            pl.BlockSpec((bm, dh), lambda b, i: (i, 0)),
            pl.BlockSpec((1, dh), lambda b, i: (0, 0)),
            pl.BlockSpec((1, dh), lambda b, i: (0, 0)),
        ],
        out_specs=[
            pl.BlockSpec((None, bm, n_q * dh), lambda b, i: (b, i, 0)),
            pl.BlockSpec((None, n_kv * dh, bm), lambda b, i: (b, 0, i)),
            pl.BlockSpec((None, bm, n_kv * dh), lambda b, i: (b, i, 0)),
        ],
        out_shape=[
            jax.ShapeDtypeStruct((batch, seq, n_q * dh), BF16),
            jax.ShapeDtypeStruct((batch, n_kv * dh, seq), BF16),
            jax.ShapeDtypeStruct((batch, seq, n_kv * dh), BF16),
        ],
        compiler_params=_params("parallel", "parallel"),
        name="qk_norm_rope",
    )(proj, cos_full, sin_signed, q_gain.reshape(1, dh), k_gain.reshape(1, dh))


def _attn_body(q_ref, kt_ref, v_ref, o_ref, vx_ref, *, group, dh, kb):
    @pl.when(pl.program_id(2) == 0)
    def _():
        vx_ref[:, 0:dh] = v_ref[...]
        vx_ref[:, dh:2 * dh] = jnp.ones((v_ref.shape[0], dh), BF16)

    n_kb = kt_ref.shape[1] // kb
    for g in range(group):
        q = q_ref[:, g * dh:(g + 1) * dh]
        m = None
        acc = None
        for j in range(n_kb):
            s = jnp.dot(q, kt_ref[:, j * kb:(j + 1) * kb], preferred_element_type=F32)
            bmax = jnp.max(s, axis=-1, keepdims=True)
            m_new = bmax if j == 0 else jnp.maximum(m, bmax)
            p = jnp.exp2(s - m_new).astype(BF16)
            pv = jnp.dot(p, vx_ref[j * kb:(j + 1) * kb, :], preferred_element_type=F32)
            acc = pv if j == 0 else acc * jnp.exp2(m - m_new) + pv
            m = m_new
        o = acc[:, 0:dh] * (1.0 / acc[:, dh:dh + 1])
        o_ref[:, g * dh:(g + 1) * dh] = o.astype(o_ref.dtype)


def attention(q, kt, v, n_kv, group, dh, tq=256, kb=256):
    b, s, _ = q.shape
    body = functools.partial(_attn_body, group=group, dh=dh, kb=min(kb, s))
    return pl.pallas_call(
        body,
        grid=(b, n_kv, s // tq),
        in_specs=[
            pl.BlockSpec((None, tq, group * dh), lambda bi, kv, i: (bi, i, kv)),
            pl.BlockSpec((None, dh, s), lambda bi, kv, i: (bi, kv, 0)),
            pl.BlockSpec((None, s, dh), lambda bi, kv, i: (bi, 0, kv)),
        ],
        out_specs=pl.BlockSpec((None, tq, group * dh), lambda bi, kv, i: (bi, i, kv)),
        out_shape=jax.ShapeDtypeStruct(q.shape, BF16),
        scratch_shapes=[pltpu.VMEM((s, 2 * dh), BF16)],
        compiler_params=_params("parallel", "parallel", "arbitrary"),
        name="attention",
    )(q, kt, v)


def _cumsum_dot(x, tri):
    hi = x.astype(BF16)
    r1 = x - hi.astype(F32)
    mid = r1.astype(BF16)
    lo = (r1 - mid.astype(F32)).astype(BF16)
    acc = jnp.dot(hi, tri, preferred_element_type=F32)
    acc += jnp.dot(mid, tri, preferred_element_type=F32)
    acc += jnp.dot(lo, tri, preferred_element_type=F32)
    return acc


def _gates_body(pre_ref, bias_ref, o_ref, *, heads, chunk_shift):
    g = pre_ref[...] + bias_ref[...]
    g = GATE_CAP * jnp.tanh(g / GATE_CAP)
    logsig = jnp.minimum(g, 0.0) - jnp.log1p(jnp.exp(-jnp.abs(g)))
    n = g.shape[1]
    t = lax.broadcasted_iota(jnp.int32, (n, n), 0)
    j = lax.broadcasted_iota(jnp.int32, (n, n), 1)
    same = lax.shift_right_logical(t, chunk_shift) == lax.shift_right_logical(j, chunk_shift)
    tri_f = jnp.where(jnp.logical_and(same, t <= j), 1.0, 0.0).astype(BF16)
    tri_b = jnp.where(jnp.logical_and(same, t >= j), 1.0, 0.0).astype(BF16)
    h = heads
    bc_f = _cumsum_dot(logsig[h:2 * h], tri_f)
    suf_b = _cumsum_dot(logsig[3 * h:4 * h], tri_b)
    o_ref[0:h, :] = bc_f
    o_ref[h:2 * h, :] = g[0:h] - bc_f
    o_ref[2 * h:3 * h, :] = suf_b
    o_ref[3 * h:4 * h, :] = g[2 * h:3 * h] - suf_b


def mlstm_gates(pre_rows, bias, heads, chunk, sb=512):
    b, r, s = pre_rows.shape
    body = functools.partial(_gates_body, heads=heads, chunk_shift=chunk.bit_length() - 1)
    return pl.pallas_call(
        body,
        grid=(b, s // sb),
        in_specs=[
            pl.BlockSpec((None, r, sb), lambda bi, i: (bi, 0, i)),
            pl.BlockSpec((r, 1), lambda bi, i: (0, 0)),
        ],
        out_specs=pl.BlockSpec((None, r, sb), lambda bi, i: (bi, 0, i)),
        out_shape=jax.ShapeDtypeStruct((b, r, s), F32),
        compiler_params=_params("parallel", "parallel"),
        name="mlstm_gates",
    )(pre_rows, bias.reshape(r, 1))


def _mlstm_chunk(q_ref, k_ref, v_ref, gcol_ref, grow_ref, h_ref, c_ref, n_ref, m_ref, hd, *, heads, q_scale, backward):
    q = q_ref[...]
    k = k_ref[...]
    v = v_ref[...]
    ln = q.shape[0]
    gcol = gcol_ref[...]
    lane = lax.broadcasted_iota(jnp.int32, gcol.shape, 1)
    base = (2 * heads if backward else 0) + hd
    bcol = jnp.sum(jnp.where(lane == base, gcol, 0.0), axis=1, keepdims=True)
    rcol = jnp.sum(jnp.where(lane == base + heads, gcol, 0.0), axis=1, keepdims=True)
    rrow = grow_ref[pl.ds(base + heads, 1), :]

    row_i = lax.broadcasted_iota(jnp.int32, (ln, ln), 0)
    col_i = lax.broadcasted_iota(jnp.int32, (ln, ln), 1)
    mask = (col_i >= row_i) if backward else (col_i <= row_i)
    dmat = jnp.where(mask, bcol + rrow, NEG)
    m_prev = m_ref[...]
    inter = bcol + m_prev
    m_row = jnp.maximum(jnp.max(dmat, axis=1, keepdims=True), inter)
    qk = lax.dot_general(q, k, (((1,), (1,)), ((), ())), preferred_element_type=F32) * q_scale
    s = qk * jnp.exp(dmat - m_row)
    decay = jnp.exp(inter - m_row)
    q_c = jnp.dot(q, c_ref[...].astype(BF16), preferred_element_type=F32) * q_scale
    num = jnp.dot(s.astype(BF16), v, preferred_element_type=F32) + decay * q_c
    q_n = jnp.sum(q.astype(F32) * n_ref[...], axis=1, keepdims=True) * q_scale
    den = jnp.sum(s, axis=1, keepdims=True) + decay * q_n
    h_ref[...] = (num / jnp.maximum(jnp.abs(den), jnp.exp(-m_row))).astype(h_ref.dtype)

    g_last = bcol[0:1, :] if backward else bcol[ln - 1:ln, :]
    wcol = g_last + rcol
    m_new = jnp.maximum(g_last + m_prev, jnp.max(wcol, axis=0, keepdims=True))
    carry_decay = jnp.exp(g_last + m_prev - m_new)
    wk = jnp.exp(wcol - m_new) * k.astype(F32)
    kv = lax.dot_general(wk.astype(BF16), v, (((0,), (0,)), ((), ())), preferred_element_type=F32)
    c_ref[...] = carry_decay * c_ref[...] + kv
    n_ref[...] = carry_decay * n_ref[...] + jnp.sum(wk, axis=0, keepdims=True)
    m_ref[...] = m_new


def _mlstm_body(qf_ref, kf_ref, vf_ref, gcf_ref, grf_ref, qb_ref, kb_ref, vb_ref, gcb_ref, grb_ref,
                hf_ref, hb_ref, c_ref, n_ref, m_ref, *, heads, q_scale):
    hd = pl.program_id(1)

    @pl.when(pl.program_id(2) == 0)
    def _():
        c_ref[...] = jnp.zeros(c_ref.shape, F32)
        n_ref[...] = jnp.zeros(n_ref.shape, F32)
        m_ref[...] = jnp.full(m_ref.shape, NEG, F32)

    chunk = functools.partial(_mlstm_chunk, heads=heads, q_scale=q_scale)
    chunk(qf_ref, kf_ref, vf_ref, gcf_ref, grf_ref, hf_ref, c_ref.at[0], n_ref.at[0], m_ref.at[0], hd,
          backward=False)
    chunk(qb_ref, kb_ref, vb_ref, gcb_ref, grb_ref, hb_ref, c_ref.at[1], n_ref.at[1], m_ref.at[1], hd,
          backward=True)


def mlstm_scan(proj, gcol, grow, heads, dk, dv, chunk):
    b, s, _ = proj.shape
    nc = s // chunk
    body = functools.partial(_mlstm_body, heads=heads, q_scale=dk ** -0.5)
    k_blk0 = heads
    v_blk0 = (2 * heads * dk) // dv

    def direction_specs(cidx):
        return [
            pl.BlockSpec((None, chunk, dk), lambda bi, h, c: (bi, cidx(c), h)),
            pl.BlockSpec((None, chunk, dk), lambda bi, h, c: (bi, cidx(c), k_blk0 + h)),
            pl.BlockSpec((None, chunk, dv), lambda bi, h, c: (bi, cidx(c), v_blk0 + h)),
            pl.BlockSpec((None, chunk, 4 * heads), lambda bi, h, c: (bi, cidx(c), 0)),
            pl.BlockSpec((None, 4 * heads, chunk), lambda bi, h, c: (bi, 0, cidx(c))),
        ]

    def fwd(c):
        return c

    def bwd(c):
        return nc - 1 - c

    out_sds = jax.ShapeDtypeStruct((b, s, heads * dv), BF16)
    return pl.pallas_call(
        body,
        grid=(b, heads, nc),
        in_specs=direction_specs(fwd) + direction_specs(bwd),
        out_specs=[
            pl.BlockSpec((None, chunk, dv), lambda bi, h, c: (bi, fwd(c), h)),
            pl.BlockSpec((None, chunk, dv), lambda bi, h, c: (bi, bwd(c), h)),
        ],
        out_shape=[out_sds, out_sds],
        scratch_shapes=[
            pltpu.VMEM((2, dk, dv), F32),
            pltpu.VMEM((2, 1, dk), F32),
            pltpu.VMEM((2, 1, 1), F32),
        ],
        compiler_params=_params("parallel", "parallel", "arbitrary"),
        name="mlstm_scan",
    )(*([proj, proj, proj, gcol, grow] * 2))


def _ml_out_body(hf_ref, hb_ref, o_ref, g_ref, out_ref, *, heads, dv):
    for h in range(heads):
        sl = slice(h * dv, (h + 1) * dv)
        x = hf_ref[:, sl].astype(F32) + hb_ref[:, sl].astype(F32)
        xn = x * lax.rsqrt(jnp.mean(x * x, axis=-1, keepdims=True) + EPS) * g_ref[:, sl]
        og = o_ref[:, sl].astype(F32)
        out_ref[:, sl] = (xn / (1.0 + jnp.exp(-og))).astype(out_ref.dtype)


def mlstm_out_gate(h_fwd, h_bwd, proj, h_gain, heads, dv, o_blk, bm=256):
    t, d = h_fwd.shape
    body = functools.partial(_ml_out_body, heads=heads, dv=dv)
    return pl.pallas_call(
        body,
        grid=(t // bm,),
        in_specs=[
            pl.BlockSpec((bm, d), lambda i: (i, 0)),
            pl.BlockSpec((bm, d), lambda i: (i, 0)),
            pl.BlockSpec((bm, d), lambda i: (i, o_blk)),
            pl.BlockSpec((1, d), lambda i: (0, 0)),
        ],
        out_specs=pl.BlockSpec((bm, d), lambda i: (i, 0)),
        out_shape=jax.ShapeDtypeStruct((t, d), BF16),
        compiler_params=_params("parallel"),
        name="mlstm_out_gate",
    )(h_fwd, h_bwd, proj, h_gain.reshape(1, d))


def _route_body(lg_ref, idx_ref, rank_ref, w_ref, cnt_ref, carry_ref):
    @pl.when(pl.program_id(0) == 0)
    def _():
        carry_ref[...] = jnp.zeros(carry_ref.shape, F32)

    lg = lg_ref[...]
    n_e, tb = lg.shape
    e_iota = lax.broadcasted_iota(jnp.int32, lg.shape, 0).astype(F32)
    t1 = jnp.max(lg, axis=0, keepdims=True)
    i1 = jnp.min(jnp.where(lg == t1, e_iota, float(n_e)), axis=0, keepdims=True)
    first = e_iota == i1
    lg2 = jnp.where(first, -jnp.inf, lg)
    t2 = jnp.max(lg2, axis=0, keepdims=True)
    i2 = jnp.min(jnp.where(lg2 == t2, e_iota, float(n_e)), axis=0, keepdims=True)
    second = e_iota == i2
    e2 = jnp.exp(t2 - t1)
    w_ref[0:1, :] = 1.0 / (1.0 + e2)
    w_ref[1:2, :] = e2 / (1.0 + e2)
    idx_ref[0:1, :] = i1.astype(jnp.int32)
    idx_ref[1:2, :] = i2.astype(jnp.int32)

    assign = jnp.where(first, 1.0, 0.0) + jnp.where(second, 1.0, 0.0)
    tp = lax.broadcasted_iota(jnp.int32, (tb, tb), 0)
    tc = lax.broadcasted_iota(jnp.int32, (tb, tb), 1)
    before = jnp.where(tp < tc, 1.0, 0.0).astype(BF16)
    rank = jnp.dot(assign.astype(BF16), before, preferred_element_type=F32) + carry_ref[:, 0:1]
    rank_ref[0:1, :] = jnp.sum(jnp.where(first, rank, 0.0), axis=0, keepdims=True).astype(jnp.int32)
    rank_ref[1:2, :] = jnp.sum(jnp.where(second, rank, 0.0), axis=0, keepdims=True).astype(jnp.int32)
    carry_ref[...] = carry_ref[...] + jnp.sum(assign, axis=1, keepdims=True)
    cnt_ref[...] = carry_ref[...]


def route_top2(logits_rows, tb=512):
    n_e, t = logits_rows.shape
    return pl.pallas_call(
        _route_body,
        grid=(t // tb,),
        in_specs=[pl.BlockSpec((n_e, tb), lambda i: (0, i))],
        out_specs=[
            pl.BlockSpec((2, tb), lambda i: (0, i)),
            pl.BlockSpec((2, tb), lambda i: (0, i)),
            pl.BlockSpec((2, tb), lambda i: (0, i)),
            pl.BlockSpec((n_e, 128), lambda i: (0, 0)),
        ],
        out_shape=[
            jax.ShapeDtypeStruct((2, t), jnp.int32),
            jax.ShapeDtypeStruct((2, t), jnp.int32),
            jax.ShapeDtypeStruct((2, t), F32),
            jax.ShapeDtypeStruct((n_e, 128), F32),
        ],
        scratch_shapes=[pltpu.VMEM((n_e, 128), F32)],
        compiler_params=_params("arbitrary"),
        name="route_top2",
    )(logits_rows)


def _gather_norm_body(src_ref, used_ref, x_hbm, g_ref, o_ref, xbuf, sem, *, tb):
    i = pl.program_id(0)
    n_used = used_ref[0]

    def row_copy(src_row, slot, j):
        return pltpu.make_async_copy(x_hbm.at[pl.ds(src_row, 1)], xbuf.at[slot, pl.ds(j, 1)], sem.at[slot])

    def issue_tile(tile):
        slot = tile % 2

        def issue(j, carry):
            row_copy(src_ref[tile * tb + j], slot, j).start()
            return carry

        lax.fori_loop(0, tb, issue, 0, unroll=ROW_DMA_UNROLL)

    @pl.when(i == 0)
    def _():
        issue_tile(i)

    @pl.when(i + 1 < n_used)
    def _():
        issue_tile(i + 1)

    @pl.when(i < n_used)
    def _():
        slot = i % 2
        pltpu.make_async_copy(x_hbm.at[pl.ds(0, tb)], xbuf.at[slot], sem.at[slot]).wait()
        x = xbuf[slot]
        inv = lax.rsqrt(jnp.mean(x * x, axis=-1, keepdims=True) + EPS)
        o_ref[...] = (x * inv * g_ref[...]).astype(o_ref.dtype)

    @pl.when(i >= n_used)
    def _():
        o_ref[...] = jnp.zeros(o_ref.shape, o_ref.dtype)


def moe_gather_norm(x, gain, src_rows, n_used, tb):
    t, d = x.shape
    r = src_rows.shape[0]
    body = functools.partial(_gather_norm_body, tb=tb)
    return pl.pallas_call(
        body,
        grid_spec=pltpu.PrefetchScalarGridSpec(
            num_scalar_prefetch=2,
            grid=(r // tb,),
            in_specs=[pl.BlockSpec(memory_space=pl.ANY), pl.BlockSpec((1, d), lambda i, src, nu: (0, 0))],
            out_specs=pl.BlockSpec((tb, d), lambda i, src, nu: (i, 0)),
            scratch_shapes=[pltpu.VMEM((2, tb, d), F32), pltpu.SemaphoreType.DMA((2,))],
        ),
        out_shape=jax.ShapeDtypeStruct((r, d), BF16),
        compiler_params=_params("arbitrary"),
        name="moe_gather_norm",
    )(src_rows, n_used, x, gain.reshape(1, d))


def _new_weight_block(exp_ref):
    i = pl.program_id(1)
    return jnp.logical_or(i == 0, exp_ref[i] != exp_ref[jnp.maximum(i - 1, 0)])


def _gmm_swiglu_body(src_ref, exp_ref, used_ref, x_ref, wg_ref, wu_ref, o_ref, wgb_ref, wub_ref):
    del src_ref

    @pl.when(_new_weight_block(exp_ref))
    def _():
        wgb_ref[...] = wg_ref[...].astype(BF16)
        wub_ref[...] = wu_ref[...].astype(BF16)

    @pl.when(pl.program_id(1) < used_ref[0])
    def _():
        a = x_ref[...]
        g = jnp.dot(a, wgb_ref[...], preferred_element_type=F32)
        u = jnp.dot(a, wub_ref[...], preferred_element_type=F32)
        o_ref[...] = _silu_mul(g, u).astype(o_ref.dtype)

    @pl.when(pl.program_id(1) >= used_ref[0])
    def _():
        o_ref[...] = jnp.zeros(o_ref.shape, o_ref.dtype)


def grouped_swiglu(xs, w13, tile_src, tile_exp, n_used, bm, bn):
    r, d = xs.shape
    f = w13.shape[2] // 2
    nb = f // bn
    return pl.pallas_call(
        _gmm_swiglu_body,
        grid_spec=pltpu.PrefetchScalarGridSpec(
            num_scalar_prefetch=3,
            grid=(nb, r // bm),
            in_specs=[
                pl.BlockSpec((bm, d), lambda j, i, src, ex, nu: (src[i], 0)),
                pl.BlockSpec((None, d, bn), lambda j, i, src, ex, nu: (ex[i], 0, j)),
                pl.BlockSpec((None, d, bn), lambda j, i, src, ex, nu: (ex[i], 0, j + nb)),
            ],
            out_specs=pl.BlockSpec((bm, bn), lambda j, i, src, ex, nu: (i, j)),
            scratch_shapes=[pltpu.VMEM((d, bn), BF16), pltpu.VMEM((d, bn), BF16)],
        ),
        out_shape=jax.ShapeDtypeStruct((r, f), BF16),
        compiler_params=_params("arbitrary", "arbitrary"),
        name="grouped_swiglu",
    )(tile_src, tile_exp, n_used, xs, w13, w13)


def _gmm_body(src_ref, exp_ref, used_ref, a_ref, w_ref, o_ref, wb_ref):
    del src_ref

    @pl.when(_new_weight_block(exp_ref))
    def _():
        wb_ref[...] = w_ref[...].astype(BF16)

    @pl.when(pl.program_id(1) < used_ref[0])
    def _():
        o_ref[...] = jnp.dot(a_ref[...], wb_ref[...], preferred_element_type=F32).astype(o_ref.dtype)

    @pl.when(pl.program_id(1) >= used_ref[0])
    def _():
        o_ref[...] = jnp.zeros(o_ref.shape, o_ref.dtype)


def grouped_matmul(a, w, tile_src, tile_exp, n_used, bm, bn):
    r, k = a.shape
    n = w.shape[2]
    return pl.pallas_call(
        _gmm_body,
        grid_spec=pltpu.PrefetchScalarGridSpec(
            num_scalar_prefetch=3,
            grid=(n // bn, r // bm),
            in_specs=[
                pl.BlockSpec((bm, k), lambda j, i, src, ex, nu: (src[i], 0)),
                pl.BlockSpec((None, k, bn), lambda j, i, src, ex, nu: (ex[i], 0, j)),
            ],
            out_specs=pl.BlockSpec((bm, bn), lambda j, i, src, ex, nu: (i, j)),
            scratch_shapes=[pltpu.VMEM((k, bn), BF16)],
        ),
        out_shape=jax.ShapeDtypeStruct((r, n), F32),
        compiler_params=_params("arbitrary", "arbitrary"),
        name="grouped_matmul",
    )(tile_src, tile_exp, n_used, a, w)


def _combine_body(pos_ref, x_ref, w_ref, y_hbm, o_ref, ybuf, sem, *, tb, t_total):
    i = pl.program_id(0)

    def row_copy(src_row, slot, choice, j):
        return pltpu.make_async_copy(
            y_hbm.at[pl.ds(src_row, 1)], ybuf.at[slot, choice, pl.ds(j, 1)], sem.at[slot])

    def issue_tile(tile):
        slot = tile % 2

        def issue(j, carry):
            t = tile * tb + j
            row_copy(pos_ref[t], slot, 0, j).start()
            row_copy(pos_ref[t_total + t], slot, 1, j).start()
            return carry

        lax.fori_loop(0, tb, issue, 0, unroll=ROW_DMA_UNROLL)

    @pl.when(i == 0)
    def _():
        issue_tile(i)

    @pl.when(i + 1 < pl.num_programs(0))
    def _():
        issue_tile(i + 1)

    slot = i % 2
    for choice in range(2):
        pltpu.make_async_copy(y_hbm.at[pl.ds(0, tb)], ybuf.at[slot, choice], sem.at[slot]).wait()
    w = w_ref[...]
    o_ref[...] = x_ref[...] + w[:, 0:1] * ybuf[slot, 0] + w[:, 1:2] * ybuf[slot, 1]


def moe_combine(x, y, pos_flat, w_cols, tb=256):
    t, d = x.shape
    body = functools.partial(_combine_body, tb=tb, t_total=t)
    return pl.pallas_call(
        body,
        grid_spec=pltpu.PrefetchScalarGridSpec(
            num_scalar_prefetch=1,
            grid=(t // tb,),
            in_specs=[
                pl.BlockSpec((tb, d), lambda i, pos: (i, 0)),
                pl.BlockSpec((tb, 2), lambda i, pos: (i, 0)),
                pl.BlockSpec(memory_space=pl.ANY),
            ],
            out_specs=pl.BlockSpec((tb, d), lambda i, pos: (i, 0)),
            scratch_shapes=[pltpu.VMEM((2, 2, tb, d), F32), pltpu.SemaphoreType.DMA((2,))],
        ),
        out_shape=jax.ShapeDtypeStruct((t, d), F32),
        compiler_params=_params("arbitrary"),
        name="moe_combine",
    )(pos_flat, x, w_cols, y)


def moe_plan(idx, rank, counts, bm, max_tiles):
    t = idx.shape[1]
    tiles = (counts + bm - 1) // bm
    tile_end = jnp.cumsum(tiles)
    row_off = (tile_end - tiles) * bm
    experts = jnp.arange(tiles.shape[0], dtype=idx.dtype)[:, None, None]
    row_base = jnp.sum(jnp.where(idx[None] == experts, row_off[:, None, None], 0), axis=0)
    pos = (row_base + rank).reshape(-1).astype(jnp.int32)
    tokens = jnp.tile(jnp.arange(t, dtype=jnp.int32), 2)
    src_rows = jnp.zeros((max_tiles * bm,), jnp.int32).at[pos].set(tokens)
    n_used = tile_end[-1]
    tile_src = jnp.minimum(jnp.arange(max_tiles, dtype=jnp.int32), n_used - 1)
    tile_exp = jnp.sum(tile_src[:, None] >= tile_end[None, :], axis=1).astype(jnp.int32)
    return pos, src_rows, tile_src, tile_exp, n_used.reshape(1).astype(jnp.int32)


def _rope_tables(seq):
    rows = seq // GRID_W
    row_ids = jnp.repeat(jnp.arange(rows, dtype=F32), GRID_W)
    col_ids = jnp.tile(jnp.arange(GRID_W, dtype=F32), rows)
    n_freq = ATT_HEAD_DIM // 4
    inv_freq = ROPE_THETA ** (-jnp.arange(n_freq, dtype=F32) / n_freq)
    ang = jnp.concatenate([row_ids[:, None] * inv_freq, col_ids[:, None] * inv_freq], axis=-1)
    cos, sin = jnp.cos(ang), jnp.sin(ang)
    cos_full = jnp.repeat(cos, 2, axis=-1)
    sin_signed = jnp.stack([-sin, sin], axis=-1).reshape(seq, ATT_HEAD_DIM)
    return cos_full, sin_signed


def _attention_layer(x, b, s, norm_g, w_in, q_gain, k_gain, w_out):
    t, d = x.shape
    hn = rmsnorm(x, norm_g, BF16)
    proj = matmul(hn, w_in, w_in.shape[1], F32, bm=1024, bn=512)
    cos_full, sin_signed = _rope_tables(s)
    q, kt, v = qk_norm_rope(proj, cos_full, sin_signed, q_gain, k_gain, b, s,
                            ATT_HEADS, ATT_KV_HEADS, ATT_HEAD_DIM)
    o = attention(q, kt, v, ATT_KV_HEADS, ATT_GROUP, ATT_HEAD_DIM)
    return matmul(o.reshape(t, -1), w_out, d, F32, bm=1024, bn=512, residual=x)


def _dense_ffn_layer(x, norm_g, w13, w2):
    hn = rmsnorm(x, norm_g, BF16)
    act = matmul_swiglu(hn, w13, bm=512, bn=512)
    return matmul_kgrid_res(act, w2, x, bm=1024, bn=1024, bk=2048)


def _mlstm_layer(x, b, s, norm_g, w_in, gate_bias, h_gain, w_out):
    t, d = x.shape
    n_main = 2 * ML_HEADS * ML_QK_DIM + 2 * ML_HEADS * ML_V_DIM
    hn = rmsnorm(x, norm_g, BF16)
    w_in_t = w_in.T
    proj = matmul(hn, w_in_t, n_main, BF16, bm=1024, bn=512, w_transposed=True)
    pre = norm_matmul_hp(x, norm_g, w_in_t, wt_rows=(n_main, 4 * ML_HEADS))
    pre_rows = pre.reshape(b, s, 4 * ML_HEADS).transpose(0, 2, 1)
    grow = mlstm_gates(pre_rows, gate_bias, ML_HEADS, ML_CHUNK)
    gcol = grow.transpose(0, 2, 1)
    h_fwd, h_bwd = mlstm_scan(proj.reshape(b, s, n_main), gcol, grow, ML_HEADS, ML_QK_DIM, ML_V_DIM, ML_CHUNK)
    o_blk = (2 * ML_HEADS * ML_QK_DIM + ML_HEADS * ML_V_DIM) // (ML_HEADS * ML_V_DIM)
    gated = mlstm_out_gate(h_fwd.reshape(t, -1), h_bwd.reshape(t, -1), proj, h_gain, ML_HEADS, ML_V_DIM, o_blk)
    return matmul(gated, w_out, d, F32, bm=1024, bn=512, residual=x)


def _moe_layer(x, norm_g, router, w13, w2):
    t, d = x.shape
    logits = norm_matmul_hp(x, norm_g, router)
    idx, rank, w_rows, cnt = route_top2(logits.T)
    max_tiles = (2 * t) // MOE_BM + N_EXPERTS
    pos, src_rows, tile_src, tile_exp, n_used = moe_plan(idx, rank, cnt[:, 0].astype(jnp.int32), MOE_BM, max_tiles)
    xs = moe_gather_norm(x, norm_g, src_rows, n_used, MOE_BM)
    act = grouped_swiglu(xs, w13, tile_src, tile_exp, n_used, MOE_BM, 512)
    y = grouped_matmul(act, w2, tile_src, tile_exp, n_used, MOE_BM, 512)
    return moe_combine(x, y, pos, w_rows.T)


def kernel(x, norm_mix, norm_ffn, att_w_in, att_q_gain, att_k_gain, att_w_out, ffn_w13, ffn_w2,
           ml_w_in, ml_gate_bias, ml_h_gain, ml_w_out, moe_router, moe_w13, moe_w2):
    b, s, d = x.shape
    h = x.reshape(b * s, d)
    h = _attention_layer(h, b, s, norm_mix[0], att_w_in[0], att_q_gain[0], att_k_gain[0], att_w_out[0])
    h = _dense_ffn_layer(h, norm_ffn[0], ffn_w13[0], ffn_w2[0])
    h = _mlstm_layer(h, b, s, norm_mix[1], ml_w_in[0], ml_gate_bias[0], ml_h_gain[0], ml_w_out[0])
    h = _moe_layer(h, norm_ffn[1], moe_router[0], moe_w13[0], moe_w2[0])
    return h.reshape(b, s, d)
```

```python
import functools

import jax
import jax.numpy as jnp
from jax import lax
from jax.experimental import pallas as pl
from jax.experimental.pallas import tpu as pltpu

F32 = jnp.float32
BF16 = jnp.bfloat16

EPS = 1e-6
NEG = -1e30
LOG2_E = 1.4426950408889634

GRID_W = 64
ROPE_THETA = 10000.0
ATT_HEADS = 32
ATT_KV_HEADS = 8
ATT_HEAD_DIM = 128
ATT_GROUP = ATT_HEADS // ATT_KV_HEADS

ML_HEADS = 8
ML_QK_DIM = 256
ML_V_DIM = 512
GATE_CAP = 15.0
ML_CHUNK = 256

N_EXPERTS = 8
MOE_BM = 384
ROW_DMA_UNROLL = 8

VMEM_LIMIT_BYTES = 56 * 1024 * 1024


def _params(*sem):
    return pltpu.CompilerParams(dimension_semantics=sem, vmem_limit_bytes=VMEM_LIMIT_BYTES)


def _rmsnorm_body(x_ref, g_ref, o_ref):
    x = x_ref[...]
    inv = lax.rsqrt(jnp.mean(x * x, axis=-1, keepdims=True) + EPS)
    o_ref[...] = (x * inv * g_ref[...]).astype(o_ref.dtype)


def rmsnorm(x, gain, out_dtype, bm=256):
    t, d = x.shape
    return pl.pallas_call(
        _rmsnorm_body,
        grid=(t // bm,),
        in_specs=[pl.BlockSpec((bm, d), lambda i: (i, 0)), pl.BlockSpec((1, d), lambda i: (0, 0))],
        out_specs=pl.BlockSpec((bm, d), lambda i: (i, 0)),
        out_shape=jax.ShapeDtypeStruct((t, d), out_dtype),
        compiler_params=_params("parallel"),
        name="rmsnorm",
    )(x, gain.reshape(1, d))


def _mm_body(a_ref, w_ref, o_ref, wb_ref):
    @pl.when(pl.program_id(1) == 0)
    def _():
        wb_ref[...] = w_ref[...].astype(BF16)

    o_ref[...] = jnp.dot(a_ref[...], wb_ref[...], preferred_element_type=F32).astype(o_ref.dtype)


def _mm_res_body(a_ref, w_ref, r_ref, o_ref, wb_ref):
    @pl.when(pl.program_id(1) == 0)
    def _():
        wb_ref[...] = w_ref[...].astype(BF16)

    o_ref[...] = r_ref[...] + jnp.dot(a_ref[...], wb_ref[...], preferred_element_type=F32)


def _mm_wt_body(a_ref, wt_ref, o_ref, wb_ref):
    @pl.when(pl.program_id(1) == 0)
    def _():
        wb_ref[...] = wt_ref[...].T.astype(BF16)

    o_ref[...] = jnp.dot(a_ref[...], wb_ref[...], preferred_element_type=F32).astype(o_ref.dtype)


def matmul(a, w, n, out_dtype, bm, bn, residual=None, w_transposed=False):
    m, k = a.shape
    in_specs = [pl.BlockSpec((bm, k), lambda j, i: (i, 0)), pl.BlockSpec((k, bn), lambda j, i: (0, j))]
    args = [a, w]
    body = _mm_body
    if w_transposed:
        assert residual is None
        in_specs[1] = pl.BlockSpec((bn, k), lambda j, i: (j, 0))
        body = _mm_wt_body
    if residual is not None:
        in_specs.append(pl.BlockSpec((bm, bn), lambda j, i: (i, j)))
        args.append(residual)
        body = _mm_res_body
    return pl.pallas_call(
        body,
        grid=(n // bn, m // bm),
        in_specs=in_specs,
        out_specs=pl.BlockSpec((bm, bn), lambda j, i: (i, j)),
        out_shape=jax.ShapeDtypeStruct((m, n), out_dtype),
        scratch_shapes=[pltpu.VMEM((k, bn), BF16)],
        compiler_params=_params("parallel", "arbitrary"),
        name="matmul",
    )(*args)


def _mm_kgrid_res_body(a_ref, w_ref, r_ref, o_ref):
    part = jnp.dot(a_ref[...], w_ref[...].astype(BF16), preferred_element_type=F32)

    @pl.when(pl.program_id(2) == 0)
    def _():
        o_ref[...] = r_ref[...] + part

    @pl.when(pl.program_id(2) != 0)
    def _():
        o_ref[...] += part


def matmul_kgrid_res(a, w, residual, bm, bn, bk):
    m, k = a.shape
    n = w.shape[1]
    return pl.pallas_call(
        _mm_kgrid_res_body,
        grid=(n // bn, m // bm, k // bk),
        in_specs=[
            pl.BlockSpec((bm, bk), lambda j, i, kk: (i, kk)),
            pl.BlockSpec((bk, bn), lambda j, i, kk: (kk, j)),
            pl.BlockSpec((bm, bn), lambda j, i, kk: (i, j)),
        ],
        out_specs=pl.BlockSpec((bm, bn), lambda j, i, kk: (i, j)),
        out_shape=jax.ShapeDtypeStruct((m, n), F32),
        compiler_params=_params("parallel", "parallel", "arbitrary"),
        name="matmul_kgrid_res",
    )(a, w, residual)


def _silu_mul(g, u):
    return g / (1.0 + jnp.exp(-g)) * u


def _mm_swiglu_body(a_ref, wg_ref, wu_ref, o_ref, wgb_ref, wub_ref):
    @pl.when(pl.program_id(1) == 0)
    def _():
        wgb_ref[...] = wg_ref[...].astype(BF16)
        wub_ref[...] = wu_ref[...].astype(BF16)

    a = a_ref[...]
    g = jnp.dot(a, wgb_ref[...], preferred_element_type=F32)
    u = jnp.dot(a, wub_ref[...], preferred_element_type=F32)
    o_ref[...] = _silu_mul(g, u).astype(o_ref.dtype)


def matmul_swiglu(a, w13, bm, bn):
    m, k = a.shape
    f = w13.shape[1] // 2
    nb = f // bn
    return pl.pallas_call(
        _mm_swiglu_body,
        grid=(nb, m // bm),
        in_specs=[
            pl.BlockSpec((bm, k), lambda j, i: (i, 0)),
            pl.BlockSpec((k, bn), lambda j, i: (0, j)),
            pl.BlockSpec((k, bn), lambda j, i: (0, j + nb)),
        ],
        out_specs=pl.BlockSpec((bm, bn), lambda j, i: (i, j)),
        out_shape=jax.ShapeDtypeStruct((m, f), BF16),
        scratch_shapes=[pltpu.VMEM((k, bn), BF16), pltpu.VMEM((k, bn), BF16)],
        compiler_params=_params("parallel", "arbitrary"),
        name="matmul_swiglu",
    )(a, w13, w13)


def _split_bf16(x):
    hi = x.astype(BF16)
    lo = (x - hi.astype(F32)).astype(BF16)
    return hi, lo


def _norm_mm_hp_body(x_ref, g_ref, w_ref, o_ref, *, w_transposed):
    x = x_ref[...]
    inv = lax.rsqrt(jnp.mean(x * x, axis=-1, keepdims=True) + EPS)
    xn = x * inv * g_ref[...]
    xh, xl = _split_bf16(xn)
    wh, wl = _split_bf16(w_ref[...])
    contract = (((1,), (1 if w_transposed else 0,)), ((), ()))
    acc = lax.dot_general(xh, wh, contract, preferred_element_type=F32)
    acc += lax.dot_general(xh, wl, contract, preferred_element_type=F32)
    acc += lax.dot_general(xl, wh, contract, preferred_element_type=F32)
    o_ref[...] = acc


def norm_matmul_hp(x, gain, w, bm=512, wt_rows=None):
    t, d = x.shape
    if wt_rows is None:
        n = w.shape[1]
        w_spec = pl.BlockSpec((d, n), lambda i: (0, 0))
    else:
        r0, n = wt_rows
        assert r0 % n == 0
        w_spec = pl.BlockSpec((n, d), lambda i: (r0 // n, 0))
    return pl.pallas_call(
        functools.partial(_norm_mm_hp_body, w_transposed=wt_rows is not None),
        grid=(t // bm,),
        in_specs=[
            pl.BlockSpec((bm, d), lambda i: (i, 0)),
            pl.BlockSpec((1, d), lambda i: (0, 0)),
            w_spec,
        ],
        out_specs=pl.BlockSpec((bm, n), lambda i: (i, 0)),
        out_shape=jax.ShapeDtypeStruct((t, n), F32),
        compiler_params=_params("parallel"),
        name="norm_matmul_hp",
    )(x, gain.reshape(1, d), w)


def _qk_rope_body(p_ref, cos_ref, sin_ref, qg_ref, kg_ref, q_ref, kt_ref, v_ref, *, n_q, n_kv, dh, q_scale):
    cos = cos_ref[...]
    sin = sin_ref[...]
    pr = lax.broadcasted_iota(jnp.int32, (dh, dh), 0)
    pc = lax.broadcasted_iota(jnp.int32, (dh, dh), 1)
    swap = jnp.where(jnp.bitwise_xor(pr, 1) == pc, 1.0, 0.0).astype(BF16)

    def norm_rope(x, gain):
        xn = x * lax.rsqrt(jnp.mean(x * x, axis=-1, keepdims=True) + EPS) * gain
        partner = jnp.dot(xn.astype(BF16), swap, preferred_element_type=F32)
        return xn * cos + partner * sin

    for h in range(n_q):
        x = p_ref[:, h * dh:(h + 1) * dh]
        q_ref[:, h * dh:(h + 1) * dh] = (norm_rope(x, qg_ref[...]) * q_scale).astype(q_ref.dtype)
    for h in range(n_kv):
        x = p_ref[:, (n_q + h) * dh:(n_q + h + 1) * dh]
        kt_ref[h * dh:(h + 1) * dh, :] = norm_rope(x, kg_ref[...]).T.astype(kt_ref.dtype)
    v0 = (n_q + n_kv) * dh
    v_ref[...] = p_ref[:, v0:v0 + n_kv * dh].astype(v_ref.dtype)


def qk_norm_rope(proj, cos_full, sin_signed, q_gain, k_gain, batch, seq, n_q, n_kv, dh, bm=256):
    sb = seq // bm
    body = functools.partial(_qk_rope_body, n_q=n_q, n_kv=n_kv, dh=dh, q_scale=dh ** -0.5 * LOG2_E)
    return pl.pallas_call(
        body,
        grid=(batch, sb),
        in_specs=[
            pl.BlockSpec((bm, proj.shape[1]), lambda b, i: (b * sb + i, 0)),
            pl.BlockSpec((bm, dh), lambda b, i: (i, 0)),
            pl.BlockSpec((bm, dh), lambda b, i: (i, 0)),
            pl.BlockSpec((1, dh), lambda b, i: (0, 0)),
            pl.BlockSpec((1, dh), lambda b, i: (0, 0)),
        ],
        out_specs=[
            pl.BlockSpec((None, bm, n_q * dh), lambda b, i: (b, i, 0)),
            pl.BlockSpec((None, n_kv * dh, bm), lambda b, i: (b, 0, i)),
            pl.BlockSpec((None, bm, n_kv * dh), lambda b, i: (b, i, 0)),
        ],
        out_shape=[
            jax.ShapeDtypeStruct((batch, seq, n_q * dh), BF16),
            jax.ShapeDtypeStruct((batch, n_kv * dh, seq), BF16),
            jax.ShapeDtypeStruct((batch, seq, n_kv * dh), BF16),
        ],
        compiler_params=_params("parallel", "parallel"),
        name="qk_norm_rope",
    )(proj, cos_full, sin_signed, q_gain.reshape(1, dh), k_gain.reshape(1, dh))


def _attn_body(q_ref, kt_ref, v_ref, o_ref, vx_ref, *, group, dh, kb):
    @pl.when(pl.program_id(2) == 0)
    def _():
        vx_ref[:, 0:dh] = v_ref[...]
        vx_ref[:, dh:2 * dh] = jnp.ones((v_ref.shape[0], dh), BF16)

    n_kb = kt_ref.shape[1] // kb
    for g in range(group):
        q = q_ref[:, g * dh:(g + 1) * dh]
        m = None
        acc = None
        for j in range(n_kb):
            s = jnp.dot(q, kt_ref[:, j * kb:(j + 1) * kb], preferred_element_type=F32)
            bmax = jnp.max(s, axis=-1, keepdims=True)
            m_new = bmax if j == 0 else jnp.maximum(m, bmax)
            p = jnp.exp2(s - m_new).astype(BF16)
            pv = jnp.dot(p, vx_ref[j * kb:(j + 1) * kb, :], preferred_element_type=F32)
            acc = pv if j == 0 else acc * jnp.exp2(m - m_new) + pv
            m = m_new
        o = acc[:, 0:dh] * (1.0 / acc[:, dh:dh + 1])
        o_ref[:, g * dh:(g + 1) * dh] = o.astype(o_ref.dtype)


def attention(q, kt, v, n_kv, group, dh, tq=256, kb=256):
    b, s, _ = q.shape
    body = functools.partial(_attn_body, group=group, dh=dh, kb=min(kb, s))
    return pl.pallas_call(
        body,
        grid=(b, n_kv, s // tq),
        in_specs=[
            pl.BlockSpec((None, tq, group * dh), lambda bi, kv, i: (bi, i, kv)),
            pl.BlockSpec((None, dh, s), lambda bi, kv, i: (bi, kv, 0)),
            pl.BlockSpec((None, s, dh), lambda bi, kv, i: (bi, 0, kv)),
        ],
        out_specs=pl.BlockSpec((None, tq, group * dh), lambda bi, kv, i: (bi, i, kv)),
        out_shape=jax.ShapeDtypeStruct(q.shape, BF16),
        scratch_shapes=[pltpu.VMEM((s, 2 * dh), BF16)],
        compiler_params=_params("parallel", "parallel", "arbitrary"),
        name="attention",
    )(q, kt, v)


def _cumsum_dot(x, tri):
    hi = x.astype(BF16)
    r1 = x - hi.astype(F32)
    mid = r1.astype(BF16)
    lo = (r1 - mid.astype(F32)).astype(BF16)
    acc = jnp.dot(hi, tri, preferred_element_type=F32)
    acc += jnp.dot(mid, tri, preferred_element_type=F32)
    acc += jnp.dot(lo, tri, preferred_element_type=F32)
    return acc


def _gates_body(pre_ref, bias_ref, o_ref, *, heads, chunk_shift):
    g = pre_ref[...] + bias_ref[...]
    g = GATE_CAP * jnp.tanh(g / GATE_CAP)
    logsig = jnp.minimum(g, 0.0) - jnp.log1p(jnp.exp(-jnp.abs(g)))
    n = g.shape[1]
    t = lax.broadcasted_iota(jnp.int32, (n, n), 0)
    j = lax.broadcasted_iota(jnp.int32, (n, n), 1)
    same = lax.shift_right_logical(t, chunk_shift) == lax.shift_right_logical(j, chunk_shift)
    tri_f = jnp.where(jnp.logical_and(same, t <= j), 1.0, 0.0).astype(BF16)
    tri_b = jnp.where(jnp.logical_and(same, t >= j), 1.0, 0.0).astype(BF16)
    h = heads
    bc_f = _cumsum_dot(logsig[h:2 * h], tri_f)
    suf_b = _cumsum_dot(logsig[3 * h:4 * h], tri_b)
    o_ref[0:h, :] = bc_f
    o_ref[h:2 * h, :] = g[0:h] - bc_f
    o_ref[2 * h:3 * h, :] = suf_b
    o_ref[3 * h:4 * h, :] = g[2 * h:3 * h] - suf_b


def mlstm_gates(pre_rows, bias, heads, chunk, sb=512):
    b, r, s = pre_rows.shape
    body = functools.partial(_gates_body, heads=heads, chunk_shift=chunk.bit_length() - 1)
    return pl.pallas_call(
        body,
        grid=(b, s // sb),
        in_specs=[
            pl.BlockSpec((None, r, sb), lambda bi, i: (bi, 0, i)),
            pl.BlockSpec((r, 1), lambda bi, i: (0, 0)),
        ],
        out_specs=pl.BlockSpec((None, r, sb), lambda bi, i: (bi, 0, i)),
        out_shape=jax.ShapeDtypeStruct((b, r, s), F32),
        compiler_params=_params("parallel", "parallel"),
        name="mlstm_gates",
    )(pre_rows, bias.reshape(r, 1))


def _mlstm_chunk(q_ref, k_ref, v_ref, gcol_ref, grow_ref, h_ref, c_ref, n_ref, m_ref, hd, *, heads, q_scale, backward):
    q = q_ref[...]
    k = k_ref[...]
    v = v_ref[...]
    ln = q.shape[0]
    gcol = gcol_ref[...]
    lane = lax.broadcasted_iota(jnp.int32, gcol.shape, 1)
    base = (2 * heads if backward else 0) + hd
    bcol = jnp.sum(jnp.where(lane == base, gcol, 0.0), axis=1, keepdims=True)
    rcol = jnp.sum(jnp.where(lane == base + heads, gcol, 0.0), axis=1, keepdims=True)
    rrow = grow_ref[pl.ds(base + heads, 1), :]

    row_i = lax.broadcasted_iota(jnp.int32, (ln, ln), 0)
    col_i = lax.broadcasted_iota(jnp.int32, (ln, ln), 1)
    mask = (col_i >= row_i) if backward else (col_i <= row_i)
    dmat = jnp.where(mask, bcol + rrow, NEG)
    m_prev = m_ref[...]
    inter = bcol + m_prev
    m_row = jnp.maximum(jnp.max(dmat, axis=1, keepdims=True), inter)
    qk = lax.dot_general(q, k, (((1,), (1,)), ((), ())), preferred_element_type=F32) * q_scale
    s = qk * jnp.exp(dmat - m_row)
    decay = jnp.exp(inter - m_row)
    q_c = jnp.dot(q, c_ref[...].astype(BF16), preferred_element_type=F32) * q_scale
    num = jnp.dot(s.astype(BF16), v, preferred_element_type=F32) + decay * q_c
    q_n = jnp.sum(q.astype(F32) * n_ref[...], axis=1, keepdims=True) * q_scale
    den = jnp.sum(s, axis=1, keepdims=True) + decay * q_n
    h_ref[...] = (num / jnp.maximum(jnp.abs(den), jnp.exp(-m_row))).astype(h_ref.dtype)

    g_last = bcol[0:1, :] if backward else bcol[ln - 1:ln, :]
    wcol = g_last + rcol
    m_new = jnp.maximum(g_last + m_prev, jnp.max(wcol, axis=0, keepdims=True))
    carry_decay = jnp.exp(g_last + m_prev - m_new)
    wk = jnp.exp(wcol - m_new) * k.astype(F32)
    kv = lax.dot_general(wk.astype(BF16), v, (((0,), (0,)), ((), ())), preferred_element_type=F32)
    c_ref[...] = carry_decay * c_ref[...] + kv
    n_ref[...] = carry_decay * n_ref[...] + jnp.sum(wk, axis=0, keepdims=True)
    m_ref[...] = m_new


def _mlstm_body(qf_ref, kf_ref, vf_ref, gcf_ref, grf_ref, qb_ref, kb_ref, vb_ref, gcb_ref, grb_ref,
                hf_ref, hb_ref, c_ref, n_ref, m_ref, *, heads, q_scale):
    hd = pl.program_id(1)

    @pl.when(pl.program_id(2) == 0)
    def _():
        c_ref[...] = jnp.zeros(c_ref.shape, F32)
        n_ref[...] = jnp.zeros(n_ref.shape, F32)
        m_ref[...] = jnp.full(m_ref.shape, NEG, F32)

    chunk = functools.partial(_mlstm_chunk, heads=heads, q_scale=q_scale)
    chunk(qf_ref, kf_ref, vf_ref, gcf_ref, grf_ref, hf_ref, c_ref.at[0], n_ref.at[0], m_ref.at[0], hd,
          backward=False)
    chunk(qb_ref, kb_ref, vb_ref, gcb_ref, grb_ref, hb_ref, c_ref.at[1], n_ref.at[1], m_ref.at[1], hd,
          backward=True)


def mlstm_scan(proj, gcol, grow, heads, dk, dv, chunk):
    b, s, _ = proj.shape
    nc = s // chunk
    body = functools.partial(_mlstm_body, heads=heads, q_scale=dk ** -0.5)
    k_blk0 = heads
    v_blk0 = (2 * heads * dk) // dv

    def direction_specs(cidx):
        return [
            pl.BlockSpec((None, chunk, dk), lambda bi, h, c: (bi, cidx(c), h)),
            pl.BlockSpec((None, chunk, dk), lambda bi, h, c: (bi, cidx(c), k_blk0 + h)),
            pl.BlockSpec((None, chunk, dv), lambda bi, h, c: (bi, cidx(c), v_blk0 + h)),
            pl.BlockSpec((None, chunk, 4 * heads), lambda bi, h, c: (bi, cidx(c), 0)),
            pl.BlockSpec((None, 4 * heads, chunk), lambda bi, h, c: (bi, 0, cidx(c))),
        ]

    def fwd(c):
        return c

    def bwd(c):
        return nc - 1 - c

    out_sds = jax.ShapeDtypeStruct((b, s, heads * dv), BF16)
    return pl.pallas_call(
        body,
        grid=(b, heads, nc),
        in_specs=direction_specs(fwd) + direction_specs(bwd),
        out_specs=[
            pl.BlockSpec((None, chunk, dv), lambda bi, h, c: (bi, fwd(c), h)),
            pl.BlockSpec((None, chunk, dv), lambda bi, h, c: (bi, bwd(c), h)),
        ],
        out_shape=[out_sds, out_sds],
        scratch_shapes=[
            pltpu.VMEM((2, dk, dv), F32),
            pltpu.VMEM((2, 1, dk), F32),
            pltpu.VMEM((2, 1, 1), F32),
        ],
        compiler_params=_params("parallel", "parallel", "arbitrary"),
        name="mlstm_scan",
    )(*([proj, proj, proj, gcol, grow] * 2))


def _ml_out_body(hf_ref, hb_ref, o_ref, g_ref, out_ref, *, heads, dv):
    for h in range(heads):
        sl = slice(h * dv, (h + 1) * dv)
        x = hf_ref[:, sl].astype(F32) + hb_ref[:, sl].astype(F32)
        xn = x * lax.rsqrt(jnp.mean(x * x, axis=-1, keepdims=True) + EPS) * g_ref[:, sl]
        og = o_ref[:, sl].astype(F32)
        out_ref[:, sl] = (xn / (1.0 + jnp.exp(-og))).astype(out_ref.dtype)


def mlstm_out_gate(h_fwd, h_bwd, proj, h_gain, heads, dv, o_blk, bm=256):
    t, d = h_fwd.shape
    body = functools.partial(_ml_out_body, heads=heads, dv=dv)
    return pl.pallas_call(
        body,
        grid=(t // bm,),
        in_specs=[
            pl.BlockSpec((bm, d), lambda i: (i, 0)),
            pl.BlockSpec((bm, d), lambda i: (i, 0)),
            pl.BlockSpec((bm, d), lambda i: (i, o_blk)),
            pl.BlockSpec((1, d), lambda i: (0, 0)),
        ],
        out_specs=pl.BlockSpec((bm, d), lambda i: (i, 0)),
        out_shape=jax.ShapeDtypeStruct((t, d), BF16),
        compiler_params=_params("parallel"),
        name="mlstm_out_gate",
    )(h_fwd, h_bwd, proj, h_gain.reshape(1, d))


def _route_body(lg_ref, idx_ref, rank_ref, w_ref, cnt_ref, carry_ref):
    @pl.when(pl.program_id(0) == 0)
    def _():
        carry_ref[...] = jnp.zeros(carry_ref.shape, F32)

    lg = lg_ref[...]
    n_e, tb = lg.shape
    e_iota = lax.broadcasted_iota(jnp.int32, lg.shape, 0).astype(F32)
    t1 = jnp.max(lg, axis=0, keepdims=True)
    i1 = jnp.min(jnp.where(lg == t1, e_iota, float(n_e)), axis=0, keepdims=True)
    first = e_iota == i1
    lg2 = jnp.where(first, -jnp.inf, lg)
    t2 = jnp.max(lg2, axis=0, keepdims=True)
    i2 = jnp.min(jnp.where(lg2 == t2, e_iota, float(n_e)), axis=0, keepdims=True)
    second = e_iota == i2
    e2 = jnp.exp(t2 - t1)
    w_ref[0:1, :] = 1.0 / (1.0 + e2)
    w_ref[1:2, :] = e2 / (1.0 + e2)
    idx_ref[0:1, :] = i1.astype(jnp.int32)
    idx_ref[1:2, :] = i2.astype(jnp.int32)

    assign = jnp.where(first, 1.0, 0.0) + jnp.where(second, 1.0, 0.0)
    tp = lax.broadcasted_iota(jnp.int32, (tb, tb), 0)
    tc = lax.broadcasted_iota(jnp.int32, (tb, tb), 1)
    before = jnp.where(tp < tc, 1.0, 0.0).astype(BF16)
    rank = jnp.dot(assign.astype(BF16), before, preferred_element_type=F32) + carry_ref[:, 0:1]
    rank_ref[0:1, :] = jnp.sum(jnp.where(first, rank, 0.0), axis=0, keepdims=True).astype(jnp.int32)
    rank_ref[1:2, :] = jnp.sum(jnp.where(second, rank, 0.0), axis=0, keepdims=True).astype(jnp.int32)
    carry_ref[...] = carry_ref[...] + jnp.sum(assign, axis=1, keepdims=True)
    cnt_ref[...] = carry_ref[...]


def route_top2(logits_rows, tb=512):
    n_e, t = logits_rows.shape
    return pl.pallas_call(
        _route_body,
        grid=(t // tb,),
        in_specs=[pl.BlockSpec((n_e, tb), lambda i: (0, i))],
        out_specs=[
            pl.BlockSpec((2, tb), lambda i: (0, i)),
            pl.BlockSpec((2, tb), lambda i: (0, i)),
            pl.BlockSpec((2, tb), lambda i: (0, i)),
            pl.BlockSpec((n_e, 128), lambda i: (0, 0)),
        ],
        out_shape=[
            jax.ShapeDtypeStruct((2, t), jnp.int32),
            jax.ShapeDtypeStruct((2, t), jnp.int32),
            jax.ShapeDtypeStruct((2, t), F32),
            jax.ShapeDtypeStruct((n_e, 128), F32),
        ],
        scratch_shapes=[pltpu.VMEM((n_e, 128), F32)],
        compiler_params=_params("arbitrary"),
        name="route_top2",
    )(logits_rows)


def _gather_norm_body(src_ref, used_ref, x_hbm, g_ref, o_ref, xbuf, sem, *, tb):
    i = pl.program_id(0)
    n_used = used_ref[0]

    def row_copy(src_row, slot, j):
        return pltpu.make_async_copy(x_hbm.at[pl.ds(src_row, 1)], xbuf.at[slot, pl.ds(j, 1)], sem.at[slot])

    def issue_tile(tile):
        slot = tile % 2

        def issue(j, carry):
            row_copy(src_ref[tile * tb + j], slot, j).start()
            return carry

        lax.fori_loop(0, tb, issue, 0, unroll=ROW_DMA_UNROLL)

    @pl.when(i == 0)
    def _():
        issue_tile(i)

    @pl.when(i + 1 < n_used)
    def _():
        issue_tile(i + 1)

    @pl.when(i < n_used)
    def _():
        slot = i % 2
        pltpu.make_async_copy(x_hbm.at[pl.ds(0, tb)], xbuf.at[slot], sem.at[slot]).wait()
        x = xbuf[slot]
        inv = lax.rsqrt(jnp.mean(x * x, axis=-1, keepdims=True) + EPS)
        o_ref[...] = (x * inv * g_ref[...]).astype(o_ref.dtype)

    @pl.when(i >= n_used)
    def _():
        o_ref[...] = jnp.zeros(o_ref.shape, o_ref.dtype)


def moe_gather_norm(x, gain, src_rows, n_used, tb):
    t, d = x.shape
    r = src_rows.shape[0]
    body = functools.partial(_gather_norm_body, tb=tb)
    return pl.pallas_call(
        body,
        grid_spec=pltpu.PrefetchScalarGridSpec(
            num_scalar_prefetch=2,
            grid=(r // tb,),
            in_specs=[pl.BlockSpec(memory_space=pl.ANY), pl.BlockSpec((1, d), lambda i, src, nu: (0, 0))],
            out_specs=pl.BlockSpec((tb, d), lambda i, src, nu: (i, 0)),
            scratch_shapes=[pltpu.VMEM((2, tb, d), F32), pltpu.SemaphoreType.DMA((2,))],
        ),
        out_shape=jax.ShapeDtypeStruct((r, d), BF16),
        compiler_params=_params("arbitrary"),
        name="moe_gather_norm",
    )(src_rows, n_used, x, gain.reshape(1, d))


def _new_weight_block(exp_ref):
    i = pl.program_id(1)
    return jnp.logical_or(i == 0, exp_ref[i] != exp_ref[jnp.maximum(i - 1, 0)])


def _gmm_swiglu_body(src_ref, exp_ref, used_ref, x_ref, wg_ref, wu_ref, o_ref, wgb_ref, wub_ref):
    del src_ref

    @pl.when(_new_weight_block(exp_ref))
    def _():
        wgb_ref[...] = wg_ref[...].astype(BF16)
        wub_ref[...] = wu_ref[...].astype(BF16)

    @pl.when(pl.program_id(1) < used_ref[0])
    def _():
        a = x_ref[...]
        g = jnp.dot(a, wgb_ref[...], preferred_element_type=F32)
        u = jnp.dot(a, wub_ref[...], preferred_element_type=F32)
        o_ref[...] = _silu_mul(g, u).astype(o_ref.dtype)

    @pl.when(pl.program_id(1) >= used_ref[0])
    def _():
        o_ref[...] = jnp.zeros(o_ref.shape, o_ref.dtype)


def grouped_swiglu(xs, w13, tile_src, tile_exp, n_used, bm, bn):
    r, d = xs.shape
    f = w13.shape[2] // 2
    nb = f // bn
    return pl.pallas_call(
        _gmm_swiglu_body,
        grid_spec=pltpu.PrefetchScalarGridSpec(
            num_scalar_prefetch=3,
            grid=(nb, r // bm),
            in_specs=[
                pl.BlockSpec((bm, d), lambda j, i, src, ex, nu: (src[i], 0)),
                pl.BlockSpec((None, d, bn), lambda j, i, src, ex, nu: (ex[i], 0, j)),
                pl.BlockSpec((None, d, bn), lambda j, i, src, ex, nu: (ex[i], 0, j + nb)),
            ],
            out_specs=pl.BlockSpec((bm, bn), lambda j, i, src, ex, nu: (i, j)),
            scratch_shapes=[pltpu.VMEM((d, bn), BF16), pltpu.VMEM((d, bn), BF16)],
        ),
        out_shape=jax.ShapeDtypeStruct((r, f), BF16),
        compiler_params=_params("arbitrary", "arbitrary"),
        name="grouped_swiglu",
    )(tile_src, tile_exp, n_used, xs, w13, w13)


def _gmm_body(src_ref, exp_ref, used_ref, a_ref, w_ref, o_ref, wb_ref):
    del src_ref

    @pl.when(_new_weight_block(exp_ref))
    def _():
        wb_ref[...] = w_ref[...].astype(BF16)

    @pl.when(pl.program_id(1) < used_ref[0])
    def _():
        o_ref[...] = jnp.dot(a_ref[...], wb_ref[...], preferred_element_type=F32).astype(o_ref.dtype)

    @pl.when(pl.program_id(1) >= used_ref[0])
    def _():
        o_ref[...] = jnp.zeros(o_ref.shape, o_ref.dtype)


def grouped_matmul(a, w, tile_src, tile_exp, n_used, bm, bn):
    r, k = a.shape
    n = w.shape[2]
    return pl.pallas_call(
        _gmm_body,
        grid_spec=pltpu.PrefetchScalarGridSpec(
            num_scalar_prefetch=3,
            grid=(n // bn, r // bm),
            in_specs=[
                pl.BlockSpec((bm, k), lambda j, i, src, ex, nu: (src[i], 0)),
                pl.BlockSpec((None, k, bn), lambda j, i, src, ex, nu: (ex[i], 0, j)),
            ],
            out_specs=pl.BlockSpec((bm, bn), lambda j, i, src, ex, nu: (i, j)),
            scratch_shapes=[pltpu.VMEM((k, bn), BF16)],
        ),
        out_shape=jax.ShapeDtypeStruct((r, n), F32),
        compiler_params=_params("arbitrary", "arbitrary"),
        name="grouped_matmul",
    )(tile_src, tile_exp, n_used, a, w)


def _combine_body(pos_ref, x_ref, w_ref, y_hbm, o_ref, ybuf, sem, *, tb, t_total):
    i = pl.program_id(0)

    def row_copy(src_row, slot, choice, j):
        return pltpu.make_async_copy(
            y_hbm.at[pl.ds(src_row, 1)], ybuf.at[slot, choice, pl.ds(j, 1)], sem.at[slot])

    def issue_tile(tile):
        slot = tile % 2

        def issue(j, carry):
            t = tile * tb + j
            row_copy(pos_ref[t], slot, 0, j).start()
            row_copy(pos_ref[t_total + t], slot, 1, j).start()
            return carry

        lax.fori_loop(0, tb, issue, 0, unroll=ROW_DMA_UNROLL)

    @pl.when(i == 0)
    def _():
        issue_tile(i)

    @pl.when(i + 1 < pl.num_programs(0))
    def _():
        issue_tile(i + 1)

    slot = i % 2
    for choice in range(2):
        pltpu.make_async_copy(y_hbm.at[pl.ds(0, tb)], ybuf.at[slot, choice], sem.at[slot]).wait()
    w = w_ref[...]
    o_ref[...] = x_ref[...] + w[:, 0:1] * ybuf[slot, 0] + w[:, 1:2] * ybuf[slot, 1]


def moe_combine(x, y, pos_flat, w_cols, tb=256):
    t, d = x.shape
    body = functools.partial(_combine_body, tb=tb, t_total=t)
    return pl.pallas_call(
        body,
        grid_spec=pltpu.PrefetchScalarGridSpec(
            num_scalar_prefetch=1,
            grid=(t // tb,),
            in_specs=[
                pl.BlockSpec((tb, d), lambda i, pos: (i, 0)),
                pl.BlockSpec((tb, 2), lambda i, pos: (i, 0)),
                pl.BlockSpec(memory_space=pl.ANY),
            ],
            out_specs=pl.BlockSpec((tb, d), lambda i, pos: (i, 0)),
            scratch_shapes=[pltpu.VMEM((2, 2, tb, d), F32), pltpu.SemaphoreType.DMA((2,))],
        ),
        out_shape=jax.ShapeDtypeStruct((t, d), F32),
        compiler_params=_params("arbitrary"),
        name="moe_combine",
    )(pos_flat, x, w_cols, y)


def moe_plan(idx, rank, counts, bm, max_tiles):
    t = idx.shape[1]
    tiles = (counts + bm - 1) // bm
    tile_end = jnp.cumsum(tiles)
    row_off = (tile_end - tiles) * bm
    experts = jnp.arange(tiles.shape[0], dtype=idx.dtype)[:, None, None]
    row_base = jnp.sum(jnp.where(idx[None] == experts, row_off[:, None, None], 0), axis=0)
    pos = (row_base + rank).reshape(-1).astype(jnp.int32)
    tokens = jnp.tile(jnp.arange(t, dtype=jnp.int32), 2)
    src_rows = jnp.zeros((max_tiles * bm,), jnp.int32).at[pos].set(tokens)
    n_used = tile_end[-1]
    tile_src = jnp.minimum(jnp.arange(max_tiles, dtype=jnp.int32), n_used - 1)
    tile_exp = jnp.sum(tile_src[:, None] >= tile_end[None, :], axis=1).astype(jnp.int32)
    return pos, src_rows, tile_src, tile_exp, n_used.reshape(1).astype(jnp.int32)


def _rope_tables(seq):
    rows = seq // GRID_W
    row_ids = jnp.repeat(jnp.arange(rows, dtype=F32), GRID_W)
    col_ids = jnp.tile(jnp.arange(GRID_W, dtype=F32), rows)
    n_freq = ATT_HEAD_DIM // 4
    inv_freq = ROPE_THETA ** (-jnp.arange(n_freq, dtype=F32) / n_freq)
    ang = jnp.concatenate([row_ids[:, None] * inv_freq, col_ids[:, None] * inv_freq], axis=-1)
    cos, sin = jnp.cos(ang), jnp.sin(ang)
    cos_full = jnp.repeat(cos, 2, axis=-1)
    sin_signed = jnp.stack([-sin, sin], axis=-1).reshape(seq, ATT_HEAD_DIM)
    return cos_full, sin_signed


def _attention_layer(x, b, s, norm_g, w_in, q_gain, k_gain, w_out):
    t, d = x.shape
    hn = rmsnorm(x, norm_g, BF16)
    proj = matmul(hn, w_in, w_in.shape[1], F32, bm=1024, bn=512)
    cos_full, sin_signed = _rope_tables(s)
    q, kt, v = qk_norm_rope(proj, cos_full, sin_signed, q_gain, k_gain, b, s,
                            ATT_HEADS, ATT_KV_HEADS, ATT_HEAD_DIM)
    o = attention(q, kt, v, ATT_KV_HEADS, ATT_GROUP, ATT_HEAD_DIM)
    return matmul(o.reshape(t, -1), w_out, d, F32, bm=1024, bn=512, residual=x)


def _dense_ffn_layer(x, norm_g, w13, w2):
    hn = rmsnorm(x, norm_g, BF16)
    act = matmul_swiglu(hn, w13, bm=512, bn=512)
    return matmul_kgrid_res(act, w2, x, bm=1024, bn=1024, bk=2048)


def _mlstm_layer(x, b, s, norm_g, w_in, gate_bias, h_gain, w_out):
    t, d = x.shape
    n_main = 2 * ML_HEADS * ML_QK_DIM + 2 * ML_HEADS * ML_V_DIM
    hn = rmsnorm(x, norm_g, BF16)
    w_in_t = w_in.T
    proj = matmul(hn, w_in_t, n_main, BF16, bm=1024, bn=512, w_transposed=True)
    pre = norm_matmul_hp(x, norm_g, w_in_t, wt_rows=(n_main, 4 * ML_HEADS))
    pre_rows = pre.reshape(b, s, 4 * ML_HEADS).transpose(0, 2, 1)
    grow = mlstm_gates(pre_rows, gate_bias, ML_HEADS, ML_CHUNK)
    gcol = grow.transpose(0, 2, 1)
    h_fwd, h_bwd = mlstm_scan(proj.reshape(b, s, n_main), gcol, grow, ML_HEADS, ML_QK_DIM, ML_V_DIM, ML_CHUNK)
    o_blk = (2 * ML_HEADS * ML_QK_DIM + ML_HEADS * ML_V_DIM) // (ML_HEADS * ML_V_DIM)
    gated = mlstm_out_gate(h_fwd.reshape(t, -1), h_bwd.reshape(t, -1), proj, h_gain, ML_HEADS, ML_V_DIM, o_blk)
    return matmul(gated, w_out, d, F32, bm=1024, bn=512, residual=x)


def _moe_layer(x, norm_g, router, w13, w2):
    t, d = x.shape
    logits = norm_matmul_hp(x, norm_g, router)
    idx, rank, w_rows, cnt = route_top2(logits.T)
    max_tiles = (2 * t) // MOE_BM + N_EXPERTS
    pos, src_rows, tile_src, tile_exp, n_used = moe_plan(idx, rank, cnt[:, 0].astype(jnp.int32), MOE_BM, max_tiles)
    xs = moe_gather_norm(x, norm_g, src_rows, n_used, MOE_BM)
    act = grouped_swiglu(xs, w13, tile_src, tile_exp, n_used, MOE_BM, 512)
    y = grouped_matmul(act, w2, tile_src, tile_exp, n_used, MOE_BM, 512)
    return moe_combine(x, y, pos, w_rows.T)


def kernel(x, norm_mix, norm_ffn, att_w_in, att_q_gain, att_k_gain, att_w_out, ffn_w13, ffn_w2,
           ml_w_in, ml_gate_bias, ml_h_gain, ml_w_out, moe_router, moe_w13, moe_w2):
    b, s, d = x.shape
    h = x.reshape(b * s, d)
    h = _attention_layer(h, b, s, norm_mix[0], att_w_in[0], att_q_gain[0], att_k_gain[0], att_w_out[0])
    h = _dense_ffn_layer(h, norm_ffn[0], ffn_w13[0], ffn_w2[0])
    h = _mlstm_layer(h, b, s, norm_mix[1], ml_w_in[0], ml_gate_bias[0], ml_h_gain[0], ml_w_out[0])
    h = _moe_layer(h, norm_ffn[1], moe_router[0], moe_w13[0], moe_w2[0])
    return h.reshape(b, s, d)
```

```python
import functools

import jax
import jax.numpy as jnp
from jax import lax
from jax.experimental import pallas as pl
from jax.experimental.pallas import tpu as pltpu

F32 = jnp.float32
BF16 = jnp.bfloat16

EPS = 1e-6
NEG = -1e30
LOG2_E = 1.4426950408889634

GRID_W = 64
ROPE_THETA = 10000.0
ATT_HEADS = 32
ATT_KV_HEADS = 8
ATT_HEAD_DIM = 128
ATT_GROUP = ATT_HEADS // ATT_KV_HEADS

ML_HEADS = 8
ML_QK_DIM = 256
ML_V_DIM = 512
GATE_CAP = 15.0
ML_CHUNK = 256

N_EXPERTS = 8
MOE_BM = 512
GATHER_TB = 256
ROW_DMA_UNROLL = 8

VMEM_LIMIT_BYTES = 56 * 1024 * 1024


def _params(*sem):
    return pltpu.CompilerParams(dimension_semantics=sem, vmem_limit_bytes=VMEM_LIMIT_BYTES)


def _rmsnorm_body(x_ref, g_ref, o_ref):
    x = x_ref[...]
    inv = lax.rsqrt(jnp.mean(x * x, axis=-1, keepdims=True) + EPS)
    o_ref[...] = (x * inv * g_ref[...]).astype(o_ref.dtype)


def rmsnorm(x, gain, out_dtype, bm=256):
    t, d = x.shape
    return pl.pallas_call(
        _rmsnorm_body,
        grid=(t // bm,),
        in_specs=[pl.BlockSpec((bm, d), lambda i: (i, 0)), pl.BlockSpec((1, d), lambda i: (0, 0))],
        out_specs=pl.BlockSpec((bm, d), lambda i: (i, 0)),
        out_shape=jax.ShapeDtypeStruct((t, d), out_dtype),
        compiler_params=_params("parallel"),
        name="rmsnorm",
    )(x, gain.reshape(1, d))


def _mm_body(a_ref, w_ref, o_ref, wb_ref):
    @pl.when(pl.program_id(1) == 0)
    def _():
        wb_ref[...] = w_ref[...].astype(BF16)

    o_ref[...] = jnp.dot(a_ref[...], wb_ref[...], preferred_element_type=F32).astype(o_ref.dtype)


def _mm_res_body(a_ref, w_ref, r_ref, o_ref, wb_ref):
    @pl.when(pl.program_id(1) == 0)
    def _():
        wb_ref[...] = w_ref[...].astype(BF16)

    o_ref[...] = r_ref[...] + jnp.dot(a_ref[...], wb_ref[...], preferred_element_type=F32)


def _mm_wt_body(a_ref, wt_ref, o_ref, wb_ref):
    @pl.when(pl.program_id(1) == 0)
    def _():
        wb_ref[...] = wt_ref[...].T.astype(BF16)

    o_ref[...] = jnp.dot(a_ref[...], wb_ref[...], preferred_element_type=F32).astype(o_ref.dtype)


def matmul(a, w, n, out_dtype, bm, bn, residual=None, w_transposed=False):
    m, k = a.shape
    in_specs = [pl.BlockSpec((bm, k), lambda j, i: (i, 0)), pl.BlockSpec((k, bn), lambda j, i: (0, j))]
    args = [a, w]
    body = _mm_body
    if w_transposed:
        assert residual is None
        in_specs[1] = pl.BlockSpec((bn, k), lambda j, i: (j, 0))
        body = _mm_wt_body
    if residual is not None:
        in_specs.append(pl.BlockSpec((bm, bn), lambda j, i: (i, j)))
        args.append(residual)
        body = _mm_res_body
    return pl.pallas_call(
        body,
        grid=(n // bn, m // bm),
        in_specs=in_specs,
        out_specs=pl.BlockSpec((bm, bn), lambda j, i: (i, j)),
        out_shape=jax.ShapeDtypeStruct((m, n), out_dtype),
        scratch_shapes=[pltpu.VMEM((k, bn), BF16)],
        compiler_params=_params("parallel", "arbitrary"),
        name="matmul",
    )(*args)


def _mm_kgrid_res_body(a_ref, w_ref, r_ref, o_ref):
    part = jnp.dot(a_ref[...], w_ref[...].astype(BF16), preferred_element_type=F32)

    @pl.when(pl.program_id(2) == 0)
    def _():
        o_ref[...] = r_ref[...] + part

    @pl.when(pl.program_id(2) != 0)
    def _():
        o_ref[...] += part


def matmul_kgrid_res(a, w, residual, bm, bn, bk):
    m, k = a.shape
    n = w.shape[1]
    return pl.pallas_call(
        _mm_kgrid_res_body,
        grid=(n // bn, m // bm, k // bk),
        in_specs=[
            pl.BlockSpec((bm, bk), lambda j, i, kk: (i, kk)),
            pl.BlockSpec((bk, bn), lambda j, i, kk: (kk, j)),
            pl.BlockSpec((bm, bn), lambda j, i, kk: (i, j)),
        ],
        out_specs=pl.BlockSpec((bm, bn), lambda j, i, kk: (i, j)),
        out_shape=jax.ShapeDtypeStruct((m, n), F32),
        compiler_params=_params("parallel", "parallel", "arbitrary"),
        name="matmul_kgrid_res",
    )(a, w, residual)


def _silu_mul(g, u):
    return g / (1.0 + jnp.exp(-g)) * u


def _mm_swiglu_body(a_ref, wg_ref, wu_ref, o_ref, wgb_ref, wub_ref):
    @pl.when(pl.program_id(1) == 0)
    def _():
        wgb_ref[...] = wg_ref[...].astype(BF16)
        wub_ref[...] = wu_ref[...].astype(BF16)

    a = a_ref[...]
    g = jnp.dot(a, wgb_ref[...], preferred_element_type=F32)
    u = jnp.dot(a, wub_ref[...], preferred_element_type=F32)
    o_ref[...] = _silu_mul(g, u).astype(o_ref.dtype)


def matmul_swiglu(a, w13, bm, bn):
    m, k = a.shape
    f = w13.shape[1] // 2
    nb = f // bn
    return pl.pallas_call(
        _mm_swiglu_body,
        grid=(nb, m // bm),
        in_specs=[
            pl.BlockSpec((bm, k), lambda j, i: (i, 0)),
            pl.BlockSpec((k, bn), lambda j, i: (0, j)),
            pl.BlockSpec((k, bn), lambda j, i: (0, j + nb)),
        ],
        out_specs=pl.BlockSpec((bm, bn), lambda j, i: (i, j)),
        out_shape=jax.ShapeDtypeStruct((m, f), BF16),
        scratch_shapes=[pltpu.VMEM((k, bn), BF16), pltpu.VMEM((k, bn), BF16)],
        compiler_params=_params("parallel", "arbitrary"),
        name="matmul_swiglu",
    )(a, w13, w13)


def _split_bf16(x):
    hi = x.astype(BF16)
    lo = (x - hi.astype(F32)).astype(BF16)
    return hi, lo


def _norm_mm_hp_body(x_ref, g_ref, w_ref, o_ref, *, w_transposed):
    x = x_ref[...]
    inv = lax.rsqrt(jnp.mean(x * x, axis=-1, keepdims=True) + EPS)
    xn = x * inv * g_ref[...]
    xh, xl = _split_bf16(xn)
    wh, wl = _split_bf16(w_ref[...])
    contract = (((1,), (1 if w_transposed else 0,)), ((), ()))
    acc = lax.dot_general(xh, wh, contract, preferred_element_type=F32)
    acc += lax.dot_general(xh, wl, contract, preferred_element_type=F32)
    acc += lax.dot_general(xl, wh, contract, preferred_element_type=F32)
    o_ref[...] = acc


def norm_matmul_hp(x, gain, w, bm=512, wt_rows=None):
    t, d = x.shape
    if wt_rows is None:
        n = w.shape[1]
        w_spec = pl.BlockSpec((d, n), lambda i: (0, 0))
    else:
        r0, n = wt_rows
        assert r0 % n == 0
        w_spec = pl.BlockSpec((n, d), lambda i: (r0 // n, 0))
    return pl.pallas_call(
        functools.partial(_norm_mm_hp_body, w_transposed=wt_rows is not None),
        grid=(t // bm,),
        in_specs=[
            pl.BlockSpec((bm, d), lambda i: (i, 0)),
            pl.BlockSpec((1, d), lambda i: (0, 0)),
            w_spec,
        ],
        out_specs=pl.BlockSpec((bm, n), lambda i: (i, 0)),
        out_shape=jax.ShapeDtypeStruct((t, n), F32),
        compiler_params=_params("parallel"),
        name="norm_matmul_hp",
    )(x, gain.reshape(1, d), w)


def _qk_rope_body(p_ref, cos_ref, sin_ref, qg_ref, kg_ref, q_ref, kt_ref, v_ref, *, n_q, n_kv, dh, q_scale):
    cos = cos_ref[...]
    sin = sin_ref[...]
    pr = lax.broadcasted_iota(jnp.int32, (dh, dh), 0)
    pc = lax.broadcasted_iota(jnp.int32, (dh, dh), 1)
    swap = jnp.where(jnp.bitwise_xor(pr, 1) == pc, 1.0, 0.0).astype(BF16)

    def norm_rope(x, gain):
        xn = x * lax.rsqrt(jnp.mean(x * x, axis=-1, keepdims=True) + EPS) * gain
        partner = jnp.dot(xn.astype(BF16), swap, preferred_element_type=F32)
        return xn * cos + partner * sin

    for h in range(n_q):
        x = p_ref[:, h * dh:(h + 1) * dh]
        q_ref[:, h * dh:(h + 1) * dh] = (norm_rope(x, qg_ref[...]) * q_scale).astype(q_ref.dtype)
    for h in range(n_kv):
        x = p_ref[:, (n_q + h) * dh:(n_q + h + 1) * dh]
        kt_ref[h * dh:(h + 1) * dh, :] = norm_rope(x, kg_ref[...]).T.astype(kt_ref.dtype)
    v0 = (n_q + n_kv) * dh
    v_ref[...] = p_ref[:, v0:v0 + n_kv * dh].astype(v_ref.dtype)


def qk_norm_rope(proj, cos_full, sin_signed, q_gain, k_gain, batch, seq, n_q, n_kv, dh, bm=256):
    sb = seq // bm
    body = functools.partial(_qk_rope_body, n_q=n_q, n_kv=n_kv, dh=dh, q_scale=dh ** -0.5 * LOG2_E)
    return pl.pallas_call(
        body,
        grid=(batch, sb),
        in_specs=[
            pl.BlockSpec((bm, proj.shape[1]), lambda b, i: (b * sb + i, 0)),
            pl.BlockSpec((bm, dh), lambda b, i: (i, 0)),
            pl.BlockSpec((bm, dh), lambda b, i: (i, 0)),
            pl.BlockSpec((1, dh), lambda b, i: (0, 0)),
            pl.BlockSpec((1, dh), lambda b, i: (0, 0)),
        ],
        out_specs=[
            pl.BlockSpec((None, bm, n_q * dh), lambda b, i: (b, i, 0)),
            pl.BlockSpec((None, n_kv * dh, bm), lambda b, i: (b, 0, i)),
            pl.BlockSpec((None, bm, n_kv * dh), lambda b, i: (b, i, 0)),
        ],
        out_shape=[
            jax.ShapeDtypeStruct((batch, seq, n_q * dh), BF16),
            jax.ShapeDtypeStruct((batch, n_kv * dh, seq), BF16),
            jax.ShapeDtypeStruct((batch, seq, n_kv * dh), BF16),
        ],
        compiler_params=_params("parallel", "parallel"),
        name="qk_norm_rope",
    )(proj, cos_full, sin_signed, q_gain.reshape(1, dh), k_gain.reshape(1, dh))


def _attn_body(q_ref, kt_ref, v_ref, o_ref, vx_ref, *, group, dh, kb):
    @pl.when(pl.program_id(2) == 0)
    def _():
        vx_ref[:, 0:dh] = v_ref[...]
        vx_ref[:, dh:2 * dh] = jnp.ones((v_ref.shape[0], dh), BF16)

    n_kb = kt_ref.shape[1] // kb
    for g in range(group):
        q = q_ref[:, g * dh:(g + 1) * dh]
        m = None
        acc = None
        for j in range(n_kb):
            s = jnp.dot(q, kt_ref[:, j * kb:(j + 1) * kb], preferred_element_type=F32)
            bmax = jnp.max(s, axis=-1, keepdims=True)
            m_new = bmax if j == 0 else jnp.maximum(m, bmax)
            p = jnp.exp2(s - m_new).astype(BF16)
            pv = jnp.dot(p, vx_ref[j * kb:(j + 1) * kb, :], preferred_element_type=F32)
            acc = pv if j == 0 else acc * jnp.exp2(m - m_new) + pv
            m = m_new
        o = acc[:, 0:dh] * (1.0 / acc[:, dh:dh + 1])
        o_ref[:, g * dh:(g + 1) * dh] = o.astype(o_ref.dtype)


def attention(q, kt, v, n_kv, group, dh, tq=256, kb=256):
    b, s, _ = q.shape
    body = functools.partial(_attn_body, group=group, dh=dh, kb=min(kb, s))
    return pl.pallas_call(
        body,
        grid=(b, n_kv, s // tq),
        in_specs=[
            pl.BlockSpec((None, tq, group * dh), lambda bi, kv, i: (bi, i, kv)),
            pl.BlockSpec((None, dh, s), lambda bi, kv, i: (bi, kv, 0)),
            pl.BlockSpec((None, s, dh), lambda bi, kv, i: (bi, 0, kv)),
        ],
        out_specs=pl.BlockSpec((None, tq, group * dh), lambda bi, kv, i: (bi, i, kv)),
        out_shape=jax.ShapeDtypeStruct(q.shape, BF16),
        scratch_shapes=[pltpu.VMEM((s, 2 * dh), BF16)],
        compiler_params=_params("parallel", "parallel", "arbitrary"),
        name="attention",
    )(q, kt, v)


def _cumsum_dot(x, tri):
    hi = x.astype(BF16)
    r1 = x - hi.astype(F32)
    mid = r1.astype(BF16)
    lo = (r1 - mid.astype(F32)).astype(BF16)
    acc = jnp.dot(hi, tri, preferred_element_type=F32)
    acc += jnp.dot(mid, tri, preferred_element_type=F32)
    acc += jnp.dot(lo, tri, preferred_element_type=F32)
    return acc


def _gates_body(pre_ref, bias_ref, o_ref, *, heads, chunk_shift):
    g = pre_ref[...] + bias_ref[...]
    g = GATE_CAP * jnp.tanh(g / GATE_CAP)
    logsig = jnp.minimum(g, 0.0) - jnp.log1p(jnp.exp(-jnp.abs(g)))
    n = g.shape[1]
    t = lax.broadcasted_iota(jnp.int32, (n, n), 0)
    j = lax.broadcasted_iota(jnp.int32, (n, n), 1)
    same = lax.shift_right_logical(t, chunk_shift) == lax.shift_right_logical(j, chunk_shift)
    tri_f = jnp.where(jnp.logical_and(same, t <= j), 1.0, 0.0).astype(BF16)
    tri_b = jnp.where(jnp.logical_and(same, t >= j), 1.0, 0.0).astype(BF16)
    h = heads
    bc_f = _cumsum_dot(logsig[h:2 * h], tri_f)
    suf_b = _cumsum_dot(logsig[3 * h:4 * h], tri_b)
    o_ref[0:h, :] = bc_f
    o_ref[h:2 * h, :] = g[0:h] - bc_f
    o_ref[2 * h:3 * h, :] = suf_b
    o_ref[3 * h:4 * h, :] = g[2 * h:3 * h] - suf_b


def mlstm_gates(pre_rows, bias, heads, chunk, sb=512):
    b, r, s = pre_rows.shape
    body = functools.partial(_gates_body, heads=heads, chunk_shift=chunk.bit_length() - 1)
    return pl.pallas_call(
        body,
        grid=(b, s // sb),
        in_specs=[
            pl.BlockSpec((None, r, sb), lambda bi, i: (bi, 0, i)),
            pl.BlockSpec((r, 1), lambda bi, i: (0, 0)),
        ],
        out_specs=pl.BlockSpec((None, r, sb), lambda bi, i: (bi, 0, i)),
        out_shape=jax.ShapeDtypeStruct((b, r, s), F32),
        compiler_params=_params("parallel", "parallel"),
        name="mlstm_gates",
    )(pre_rows, bias.reshape(r, 1))


def _mlstm_chunk(q_ref, k_ref, v_ref, gcol_ref, grow_ref, h_ref, c_ref, n_ref, m_ref, hd, *, heads, q_scale, backward):
    q = q_ref[...]
    k = k_ref[...]
    v = v_ref[...]
    ln = q.shape[0]
    gcol = gcol_ref[...]
    lane = lax.broadcasted_iota(jnp.int32, gcol.shape, 1)
    base = (2 * heads if backward else 0) + hd
    bcol = jnp.sum(jnp.where(lane == base, gcol, 0.0), axis=1, keepdims=True)
    rcol = jnp.sum(jnp.where(lane == base + heads, gcol, 0.0), axis=1, keepdims=True)
    rrow = grow_ref[pl.ds(base + heads, 1), :]

    row_i = lax.broadcasted_iota(jnp.int32, (ln, ln), 0)
    col_i = lax.broadcasted_iota(jnp.int32, (ln, ln), 1)
    mask = (col_i >= row_i) if backward else (col_i <= row_i)
    dmat = jnp.where(mask, bcol + rrow, NEG)
    m_prev = m_ref[...]
    inter = bcol + m_prev
    m_row = jnp.maximum(jnp.max(dmat, axis=1, keepdims=True), inter)
    qk = lax.dot_general(q, k, (((1,), (1,)), ((), ())), preferred_element_type=F32) * q_scale
    s = qk * jnp.exp(dmat - m_row)
    decay = jnp.exp(inter - m_row)
    q_c = jnp.dot(q, c_ref[...].astype(BF16), preferred_element_type=F32) * q_scale
    num = jnp.dot(s.astype(BF16), v, preferred_element_type=F32) + decay * q_c
    q_n = jnp.sum(q.astype(F32) * n_ref[...], axis=1, keepdims=True) * q_scale
    den = jnp.sum(s, axis=1, keepdims=True) + decay * q_n
    h_ref[...] = (num / jnp.maximum(jnp.abs(den), jnp.exp(-m_row))).astype(h_ref.dtype)

    g_last = bcol[0:1, :] if backward else bcol[ln - 1:ln, :]
    wcol = g_last + rcol
    m_new = jnp.maximum(g_last + m_prev, jnp.max(wcol, axis=0, keepdims=True))
    carry_decay = jnp.exp(g_last + m_prev - m_new)
    wk = jnp.exp(wcol - m_new) * k.astype(F32)
    kv = lax.dot_general(wk.astype(BF16), v, (((0,), (0,)), ((), ())), preferred_element_type=F32)
    c_ref[...] = carry_decay * c_ref[...] + kv
    n_ref[...] = carry_decay * n_ref[...] + jnp.sum(wk, axis=0, keepdims=True)
    m_ref[...] = m_new


def _mlstm_body(qf_ref, kf_ref, vf_ref, gcf_ref, grf_ref, qb_ref, kb_ref, vb_ref, gcb_ref, grb_ref,
                hf_ref, hb_ref, c_ref, n_ref, m_ref, *, heads, q_scale):
    hd = pl.program_id(1)

    @pl.when(pl.program_id(2) == 0)
    def _():
        c_ref[...] = jnp.zeros(c_ref.shape, F32)
        n_ref[...] = jnp.zeros(n_ref.shape, F32)
        m_ref[...] = jnp.full(m_ref.shape, NEG, F32)

    chunk = functools.partial(_mlstm_chunk, heads=heads, q_scale=q_scale)
    chunk(qf_ref, kf_ref, vf_ref, gcf_ref, grf_ref, hf_ref, c_ref.at[0], n_ref.at[0], m_ref.at[0], hd,
          backward=False)
    chunk(qb_ref, kb_ref, vb_ref, gcb_ref, grb_ref, hb_ref, c_ref.at[1], n_ref.at[1], m_ref.at[1], hd,
          backward=True)


def mlstm_scan(proj, gcol, grow, heads, dk, dv, chunk):
    b, s, _ = proj.shape
    nc = s // chunk
    body = functools.partial(_mlstm_body, heads=heads, q_scale=dk ** -0.5)
    k_blk0 = heads
    v_blk0 = (2 * heads * dk) // dv

    def direction_specs(cidx):
        return [
            pl.BlockSpec((None, chunk, dk), lambda bi, h, c: (bi, cidx(c), h)),
            pl.BlockSpec((None, chunk, dk), lambda bi, h, c: (bi, cidx(c), k_blk0 + h)),
            pl.BlockSpec((None, chunk, dv), lambda bi, h, c: (bi, cidx(c), v_blk0 + h)),
            pl.BlockSpec((None, chunk, 4 * heads), lambda bi, h, c: (bi, cidx(c), 0)),
            pl.BlockSpec((None, 4 * heads, chunk), lambda bi, h, c: (bi, 0, cidx(c))),
        ]

    def fwd(c):
        return c

    def bwd(c):
        return nc - 1 - c

    out_sds = jax.ShapeDtypeStruct((b, s, heads * dv), BF16)
    return pl.pallas_call(
        body,
        grid=(b, heads, nc),
        in_specs=direction_specs(fwd) + direction_specs(bwd),
        out_specs=[
            pl.BlockSpec((None, chunk, dv), lambda bi, h, c: (bi, fwd(c), h)),
            pl.BlockSpec((None, chunk, dv), lambda bi, h, c: (bi, bwd(c), h)),
        ],
        out_shape=[out_sds, out_sds],
        scratch_shapes=[
            pltpu.VMEM((2, dk, dv), F32),
            pltpu.VMEM((2, 1, dk), F32),
            pltpu.VMEM((2, 1, 1), F32),
        ],
        compiler_params=_params("parallel", "parallel", "arbitrary"),
        name="mlstm_scan",
    )(*([proj, proj, proj, gcol, grow] * 2))


def _ml_out_body(hf_ref, hb_ref, o_ref, g_ref, out_ref, *, heads, dv):
    for h in range(heads):
        sl = slice(h * dv, (h + 1) * dv)
        x = hf_ref[:, sl].astype(F32) + hb_ref[:, sl].astype(F32)
        xn = x * lax.rsqrt(jnp.mean(x * x, axis=-1, keepdims=True) + EPS) * g_ref[:, sl]
        og = o_ref[:, sl].astype(F32)
        out_ref[:, sl] = (xn / (1.0 + jnp.exp(-og))).astype(out_ref.dtype)


def mlstm_out_gate(h_fwd, h_bwd, proj, h_gain, heads, dv, o_blk, bm=256):
    t, d = h_fwd.shape
    body = functools.partial(_ml_out_body, heads=heads, dv=dv)
    return pl.pallas_call(
        body,
        grid=(t // bm,),
        in_specs=[
            pl.BlockSpec((bm, d), lambda i: (i, 0)),
            pl.BlockSpec((bm, d), lambda i: (i, 0)),
            pl.BlockSpec((bm, d), lambda i: (i, o_blk)),
            pl.BlockSpec((1, d), lambda i: (0, 0)),
        ],
        out_specs=pl.BlockSpec((bm, d), lambda i: (i, 0)),
        out_shape=jax.ShapeDtypeStruct((t, d), BF16),
        compiler_params=_params("parallel"),
        name="mlstm_out_gate",
    )(h_fwd, h_bwd, proj, h_gain.reshape(1, d))


def _route_body(lg_ref, idx_ref, rank_ref, w_ref, cnt_ref, carry_ref):
    @pl.when(pl.program_id(0) == 0)
    def _():
        carry_ref[...] = jnp.zeros(carry_ref.shape, F32)

    lg = lg_ref[...]
    n_e, tb = lg.shape
    e_iota = lax.broadcasted_iota(jnp.int32, lg.shape, 0).astype(F32)
    t1 = jnp.max(lg, axis=0, keepdims=True)
    i1 = jnp.min(jnp.where(lg == t1, e_iota, float(n_e)), axis=0, keepdims=True)
    first = e_iota == i1
    lg2 = jnp.where(first, -jnp.inf, lg)
    t2 = jnp.max(lg2, axis=0, keepdims=True)
    i2 = jnp.min(jnp.where(lg2 == t2, e_iota, float(n_e)), axis=0, keepdims=True)
    second = e_iota == i2
    e2 = jnp.exp(t2 - t1)
    w_ref[0:1, :] = 1.0 / (1.0 + e2)
    w_ref[1:2, :] = e2 / (1.0 + e2)
    idx_ref[0:1, :] = i1.astype(jnp.int32)
    idx_ref[1:2, :] = i2.astype(jnp.int32)

    assign = jnp.where(first, 1.0, 0.0) + jnp.where(second, 1.0, 0.0)
    tp = lax.broadcasted_iota(jnp.int32, (tb, tb), 0)
    tc = lax.broadcasted_iota(jnp.int32, (tb, tb), 1)
    before = jnp.where(tp < tc, 1.0, 0.0).astype(BF16)
    rank = jnp.dot(assign.astype(BF16), before, preferred_element_type=F32) + carry_ref[:, 0:1]
    rank_ref[0:1, :] = jnp.sum(jnp.where(first, rank, 0.0), axis=0, keepdims=True).astype(jnp.int32)
    rank_ref[1:2, :] = jnp.sum(jnp.where(second, rank, 0.0), axis=0, keepdims=True).astype(jnp.int32)
    carry_ref[...] = carry_ref[...] + jnp.sum(assign, axis=1, keepdims=True)
    cnt_ref[...] = carry_ref[...]


def route_top2(logits_rows, tb=512):
    n_e, t = logits_rows.shape
    return pl.pallas_call(
        _route_body,
        grid=(t // tb,),
        in_specs=[pl.BlockSpec((n_e, tb), lambda i: (0, i))],
        out_specs=[
            pl.BlockSpec((2, tb), lambda i: (0, i)),
            pl.BlockSpec((2, tb), lambda i: (0, i)),
            pl.BlockSpec((2, tb), lambda i: (0, i)),
            pl.BlockSpec((n_e, 128), lambda i: (0, 0)),
        ],
        out_shape=[
            jax.ShapeDtypeStruct((2, t), jnp.int32),
            jax.ShapeDtypeStruct((2, t), jnp.int32),
            jax.ShapeDtypeStruct((2, t), F32),
            jax.ShapeDtypeStruct((n_e, 128), F32),
        ],
        scratch_shapes=[pltpu.VMEM((n_e, 128), F32)],
        compiler_params=_params("arbitrary"),
        name="route_top2",
    )(logits_rows)


def _gather_norm_body(src_ref, used_ref, x_hbm, g_ref, o_ref, xbuf, sem, *, tb):
    i = pl.program_id(0)
    n_used = used_ref[0]

    def row_copy(src_row, slot, j):
        return pltpu.make_async_copy(x_hbm.at[pl.ds(src_row, 1)], xbuf.at[slot, pl.ds(j, 1)], sem.at[slot])

    def issue_tile(tile):
        slot = tile % 2

        def issue(j, carry):
            row_copy(src_ref[tile * tb + j], slot, j).start()
            return carry

        lax.fori_loop(0, tb, issue, 0, unroll=ROW_DMA_UNROLL)

    @pl.when(i == 0)
    def _():
        issue_tile(i)

    @pl.when(i + 1 < n_used)
    def _():
        issue_tile(i + 1)

    @pl.when(i < n_used)
    def _():
        slot = i % 2
        pltpu.make_async_copy(x_hbm.at[pl.ds(0, tb)], xbuf.at[slot], sem.at[slot]).wait()
        x = xbuf[slot]
        inv = lax.rsqrt(jnp.mean(x * x, axis=-1, keepdims=True) + EPS)
        o_ref[...] = (x * inv * g_ref[...]).astype(o_ref.dtype)

    @pl.when(i >= n_used)
    def _():
        o_ref[...] = jnp.zeros(o_ref.shape, o_ref.dtype)


def moe_gather_norm(x, gain, src_rows, n_used, tb):
    t, d = x.shape
    r = src_rows.shape[0]
    body = functools.partial(_gather_norm_body, tb=tb)
    return pl.pallas_call(
        body,
        grid_spec=pltpu.PrefetchScalarGridSpec(
            num_scalar_prefetch=2,
            grid=(r // tb,),
            in_specs=[pl.BlockSpec(memory_space=pl.ANY), pl.BlockSpec((1, d), lambda i, src, nu: (0, 0))],
            out_specs=pl.BlockSpec((tb, d), lambda i, src, nu: (i, 0)),
            scratch_shapes=[pltpu.VMEM((2, tb, d), F32), pltpu.SemaphoreType.DMA((2,))],
        ),
        out_shape=jax.ShapeDtypeStruct((r, d), BF16),
        compiler_params=_params("arbitrary"),
        name="moe_gather_norm",
    )(src_rows, n_used, x, gain.reshape(1, d))


def _new_weight_block(exp_ref):
    i = pl.program_id(1)
    return jnp.logical_or(i == 0, exp_ref[i] != exp_ref[jnp.maximum(i - 1, 0)])


def _gmm_swiglu_body(src_ref, exp_ref, used_ref, x_ref, wg_ref, wu_ref, o_ref, wgb_ref, wub_ref):
    del src_ref

    @pl.when(_new_weight_block(exp_ref))
    def _():
        wgb_ref[...] = wg_ref[...].astype(BF16)
        wub_ref[...] = wu_ref[...].astype(BF16)

    @pl.when(pl.program_id(1) < used_ref[0])
    def _():
        a = x_ref[...]
        g = jnp.dot(a, wgb_ref[...], preferred_element_type=F32)
        u = jnp.dot(a, wub_ref[...], preferred_element_type=F32)
        o_ref[...] = _silu_mul(g, u).astype(o_ref.dtype)

    @pl.when(pl.program_id(1) >= used_ref[0])
    def _():
        o_ref[...] = jnp.zeros(o_ref.shape, o_ref.dtype)


def grouped_swiglu(xs, w13, tile_src, tile_exp, n_used, bm, bn):
    r, d = xs.shape
    f = w13.shape[2] // 2
    nb = f // bn
    return pl.pallas_call(
        _gmm_swiglu_body,
        grid_spec=pltpu.PrefetchScalarGridSpec(
            num_scalar_prefetch=3,
            grid=(nb, r // bm),
            in_specs=[
                pl.BlockSpec((bm, d), lambda j, i, src, ex, nu: (src[i], 0)),
                pl.BlockSpec((None, d, bn), lambda j, i, src, ex, nu: (ex[i], 0, j)),
                pl.BlockSpec((None, d, bn), lambda j, i, src, ex, nu: (ex[i], 0, j + nb)),
            ],
            out_specs=pl.BlockSpec((bm, bn), lambda j, i, src, ex, nu: (i, j)),
            scratch_shapes=[pltpu.VMEM((d, bn), BF16), pltpu.VMEM((d, bn), BF16)],
        ),
        out_shape=jax.ShapeDtypeStruct((r, f), BF16),
        compiler_params=_params("arbitrary", "arbitrary"),
        name="grouped_swiglu",
    )(tile_src, tile_exp, n_used, xs, w13, w13)


def _gmm_body(src_ref, exp_ref, used_ref, a_ref, w_ref, o_ref, wb_ref):
    del src_ref

    @pl.when(_new_weight_block(exp_ref))
    def _():
        wb_ref[...] = w_ref[...].astype(BF16)

    @pl.when(pl.program_id(1) < used_ref[0])
    def _():
        o_ref[...] = jnp.dot(a_ref[...], wb_ref[...], preferred_element_type=F32).astype(o_ref.dtype)

    @pl.when(pl.program_id(1) >= used_ref[0])
    def _():
        o_ref[...] = jnp.zeros(o_ref.shape, o_ref.dtype)


def grouped_matmul(a, w, tile_src, tile_exp, n_used, bm, bn):
    r, k = a.shape
    n = w.shape[2]
    return pl.pallas_call(
        _gmm_body,
        grid_spec=pltpu.PrefetchScalarGridSpec(
            num_scalar_prefetch=3,
            grid=(n // bn, r // bm),
            in_specs=[
                pl.BlockSpec((bm, k), lambda j, i, src, ex, nu: (src[i], 0)),
                pl.BlockSpec((None, k, bn), lambda j, i, src, ex, nu: (ex[i], 0, j)),
            ],
            out_specs=pl.BlockSpec((bm, bn), lambda j, i, src, ex, nu: (i, j)),
            scratch_shapes=[pltpu.VMEM((k, bn), BF16)],
        ),
        out_shape=jax.ShapeDtypeStruct((r, n), F32),
        compiler_params=_params("arbitrary", "arbitrary"),
        name="grouped_matmul",
    )(tile_src, tile_exp, n_used, a, w)


def _combine_body(pos_ref, x_ref, w_ref, y_hbm, o_ref, ybuf, sem, *, tb, t_total):
    i = pl.program_id(0)

    def row_copy(src_row, slot, choice, j):
        return pltpu.make_async_copy(
            y_hbm.at[pl.ds(src_row, 1)], ybuf.at[slot, choice, pl.ds(j, 1)], sem.at[slot])

    def issue_tile(tile):
        slot = tile % 2

        def issue(j, carry):
            t = tile * tb + j
            row_copy(pos_ref[t], slot, 0, j).start()
            row_copy(pos_ref[t_total + t], slot, 1, j).start()
            return carry

        lax.fori_loop(0, tb, issue, 0, unroll=ROW_DMA_UNROLL)

    @pl.when(i == 0)
    def _():
        issue_tile(i)

    @pl.when(i + 1 < pl.num_programs(0))
    def _():
        issue_tile(i + 1)

    slot = i % 2
    for choice in range(2):
        pltpu.make_async_copy(y_hbm.at[pl.ds(0, tb)], ybuf.at[slot, choice], sem.at[slot]).wait()
    w = w_ref[...]
    o_ref[...] = x_ref[...] + w[:, 0:1] * ybuf[slot, 0] + w[:, 1:2] * ybuf[slot, 1]


def moe_combine(x, y, pos_flat, w_cols, tb=256):
    t, d = x.shape
    body = functools.partial(_combine_body, tb=tb, t_total=t)
    return pl.pallas_call(
        body,
        grid_spec=pltpu.PrefetchScalarGridSpec(
            num_scalar_prefetch=1,
            grid=(t // tb,),
            in_specs=[
                pl.BlockSpec((tb, d), lambda i, pos: (i, 0)),
                pl.BlockSpec((tb, 2), lambda i, pos: (i, 0)),
                pl.BlockSpec(memory_space=pl.ANY),
            ],
            out_specs=pl.BlockSpec((tb, d), lambda i, pos: (i, 0)),
            scratch_shapes=[pltpu.VMEM((2, 2, tb, d), F32), pltpu.SemaphoreType.DMA((2,))],
        ),
        out_shape=jax.ShapeDtypeStruct((t, d), F32),
        compiler_params=_params("arbitrary"),
        name="moe_combine",
    )(pos_flat, x, w_cols, y)


def moe_plan(idx, rank, counts, bm, max_tiles):
    t = idx.shape[1]
    tiles = (counts + bm - 1) // bm
    tile_end = jnp.cumsum(tiles)
    row_off = (tile_end - tiles) * bm
    experts = jnp.arange(tiles.shape[0], dtype=idx.dtype)[:, None, None]
    row_base = jnp.sum(jnp.where(idx[None] == experts, row_off[:, None, None], 0), axis=0)
    pos = (row_base + rank).reshape(-1).astype(jnp.int32)
    tokens = jnp.tile(jnp.arange(t, dtype=jnp.int32), 2)
    src_rows = jnp.zeros((max_tiles * bm,), jnp.int32).at[pos].set(tokens)
    n_used = tile_end[-1]
    tile_src = jnp.minimum(jnp.arange(max_tiles, dtype=jnp.int32), n_used - 1)
    tile_exp = jnp.sum(tile_src[:, None] >= tile_end[None, :], axis=1).astype(jnp.int32)
    return pos, src_rows, tile_src, tile_exp, n_used.reshape(1).astype(jnp.int32)


def _rope_tables(seq):
    rows = seq // GRID_W
    row_ids = jnp.repeat(jnp.arange(rows, dtype=F32), GRID_W)
    col_ids = jnp.tile(jnp.arange(GRID_W, dtype=F32), rows)
    n_freq = ATT_HEAD_DIM // 4
    inv_freq = ROPE_THETA ** (-jnp.arange(n_freq, dtype=F32) / n_freq)
    ang = jnp.concatenate([row_ids[:, None] * inv_freq, col_ids[:, None] * inv_freq], axis=-1)
    cos, sin = jnp.cos(ang), jnp.sin(ang)
    cos_full = jnp.repeat(cos, 2, axis=-1)
    sin_signed = jnp.stack([-sin, sin], axis=-1).reshape(seq, ATT_HEAD_DIM)
    return cos_full, sin_signed


def _attention_layer(x, b, s, norm_g, w_in, q_gain, k_gain, w_out):
    t, d = x.shape
    hn = rmsnorm(x, norm_g, BF16)
    proj = matmul(hn, w_in, w_in.shape[1], F32, bm=1024, bn=512)
    cos_full, sin_signed = _rope_tables(s)
    q, kt, v = qk_norm_rope(proj, cos_full, sin_signed, q_gain, k_gain, b, s,
                            ATT_HEADS, ATT_KV_HEADS, ATT_HEAD_DIM)
    o = attention(q, kt, v, ATT_KV_HEADS, ATT_GROUP, ATT_HEAD_DIM)
    return matmul(o.reshape(t, -1), w_out, d, F32, bm=1024, bn=512, residual=x)


def _dense_ffn_layer(x, norm_g, w13, w2):
    hn = rmsnorm(x, norm_g, BF16)
    act = matmul_swiglu(hn, w13, bm=512, bn=512)
    return matmul_kgrid_res(act, w2, x, bm=1024, bn=1024, bk=2048)


def _mlstm_layer(x, b, s, norm_g, w_in, gate_bias, h_gain, w_out):
    t, d = x.shape
    n_main = 2 * ML_HEADS * ML_QK_DIM + 2 * ML_HEADS * ML_V_DIM
    hn = rmsnorm(x, norm_g, BF16)
    w_in_t = w_in.T
    proj = matmul(hn, w_in_t, n_main, BF16, bm=1024, bn=512, w_transposed=True)
    pre = norm_matmul_hp(x, norm_g, w_in_t, wt_rows=(n_main, 4 * ML_HEADS))
    pre_rows = pre.reshape(b, s, 4 * ML_HEADS).transpose(0, 2, 1)
    grow = mlstm_gates(pre_rows, gate_bias, ML_HEADS, ML_CHUNK)
    gcol = grow.transpose(0, 2, 1)
    h_fwd, h_bwd = mlstm_scan(proj.reshape(b, s, n_main), gcol, grow, ML_HEADS, ML_QK_DIM, ML_V_DIM, ML_CHUNK)
    o_blk = (2 * ML_HEADS * ML_QK_DIM + ML_HEADS * ML_V_DIM) // (ML_HEADS * ML_V_DIM)
    gated = mlstm_out_gate(h_fwd.reshape(t, -1), h_bwd.reshape(t, -1), proj, h_gain, ML_HEADS, ML_V_DIM, o_blk)
    return matmul(gated, w_out, d, F32, bm=1024, bn=512, residual=x)


def _moe_layer(x, norm_g, router, w13, w2):
    t, d = x.shape
    logits = norm_matmul_hp(x, norm_g, router)
    idx, rank, w_rows, cnt = route_top2(logits.T)
    max_tiles = (2 * t) // MOE_BM + N_EXPERTS
    pos, src_rows, tile_src, tile_exp, n_used = moe_plan(idx, rank, cnt[:, 0].astype(jnp.int32), MOE_BM, max_tiles)
    xs = moe_gather_norm(x, norm_g, src_rows, n_used * (MOE_BM // GATHER_TB), GATHER_TB)
    act = grouped_swiglu(xs, w13, tile_src, tile_exp, n_used, MOE_BM, 512)
    y = grouped_matmul(act, w2, tile_src, tile_exp, n_used, MOE_BM, 512)
    return moe_combine(x, y, pos, w_rows.T)


def kernel(x, norm_mix, norm_ffn, att_w_in, att_q_gain, att_k_gain, att_w_out, ffn_w13, ffn_w2,
           ml_w_in, ml_gate_bias, ml_h_gain, ml_w_out, moe_router, moe_w13, moe_w2):
    b, s, d = x.shape
    h = x.reshape(b * s, d)
    h = _attention_layer(h, b, s, norm_mix[0], att_w_in[0], att_q_gain[0], att_k_gain[0], att_w_out[0])
    h = _dense_ffn_layer(h, norm_ffn[0], ffn_w13[0], ffn_w2[0])
    h = _mlstm_layer(h, b, s, norm_mix[1], ml_w_in[0], ml_gate_bias[0], ml_h_gain[0], ml_w_out[0])
    h = _moe_layer(h, norm_ffn[1], moe_router[0], moe_w13[0], moe_w2[0])
    return h.reshape(b, s, d)
```
